```python
import math
import jax, jax.numpy as jnp
from jax import lax
import numpy as np

D_MODEL = 2048
BATCH = 4
SEQ = 2048
DEPTH = 2
DEC_BATCH = 128
DEC_SEQ = 1
PAST_LEN = 8192
PAGE_SIZE = 128

N_EVEN = (DEPTH + 1) // 2
N_ODD = DEPTH // 2
MIX_WIDTH = D_MODEL
A_WIDTH = MIX_WIDTH // 2
A_HEADS = 8
A_HEAD_DIM = A_WIDTH // A_HEADS
CHUNK = 128
B_WIDTH = MIX_WIDTH - A_WIDTH
POOL_WINDOWS = (2, 4, 8, 16)
B_GROUPS = len(POOL_WINDOWS)
B_GROUP_DIM = B_WIDTH // B_GROUPS
POOL_HIST = max(POOL_WINDOWS) - 1
IN_WIDTH = 2 * A_WIDTH + B_WIDTH
N_HEADS = 16
QK_NOPE = 128
QK_ROPE = 64
V_HEAD = 128
Q_LORA = 512
KV_LORA = 512
LATENT = KV_LORA + QK_ROPE
ROPE_THETA = 10000.0
Q_BLOCK = 128
ATTN_SCALE = (QK_NOPE + QK_ROPE) ** -0.5
D_FF = 7168
N_EXPERTS = 8
TOP_K = 2
EPS = 1e-6

kernel_name = "hybrid_gmlp_pool_mla_moe_step"


def rmsnorm(x, g):
    xf = x.astype(jnp.float32)
    y = xf * lax.rsqrt(jnp.mean(xf * xf, axis=-1, keepdims=True) + EPS)
    return (y * g.astype(jnp.float32)).astype(x.dtype)


def swiglu(x, w_gate, w_up, w_down):
    return (jax.nn.silu(x @ w_gate) * (x @ w_up)) @ w_down


def chunk_spatial_gate(u, v, w_s, b_s):
    b, l = u.shape[:2]
    n_chunks = -(-l // CHUNK)
    pad = n_chunks * CHUNK - l
    vp = jnp.pad(v, ((0, 0), (0, pad), (0, 0), (0, 0))).reshape(b, n_chunks, CHUNK, A_HEADS, A_HEAD_DIM)
    causal = jnp.tril(jnp.ones((CHUNK, CHUNK), dtype=bool))
    w = jnp.where(causal[None], w_s, 0.0)
    mixed = jnp.einsum('hij,bcjhd->bcihd', w, vp) + jnp.swapaxes(b_s, 0, 1)[None, None, :, :, None]
    mixed = mixed.reshape(b, n_chunks * CHUNK, A_HEADS, A_HEAD_DIM)[:, :l]
    return u * mixed


def multiscale_pool(z_ext, n_new):
    lx = z_ext.shape[1]
    cs = jnp.cumsum(z_ext.astype(jnp.float32), axis=1)
    cs = jnp.pad(cs, ((0, 0), (1, 0), (0, 0), (0, 0)))
    e = jnp.arange(lx - n_new, lx)
    win = jnp.array(POOL_WINDOWS, dtype=jnp.int32)
    lo = jnp.maximum(e[:, None] + 1 - win[None, :], 0)
    cnt = (e[:, None] + 1 - lo).astype(jnp.float32)
    g = jnp.arange(B_GROUPS)[None, :]
    window_sum = cs[:, e + 1] - cs[:, lo, g]
    mean = window_sum / cnt[None, :, :, None]
    return (mean - z_ext[:, lx - n_new:].astype(jnp.float32)).astype(z_ext.dtype)


def even_mixer(h, hist, w_in, g_v, w_s, b_s, w_pool, pool_scale, w_o):
    b, l, _ = h.shape
    proj = h @ w_in
    uv = jax.nn.gelu(proj[..., :2 * A_WIDTH], approximate=False)
    u = uv[..., :A_WIDTH]
    v = rmsnorm(uv[..., A_WIDTH:], g_v)
    z = proj[..., 2 * A_WIDTH:]
    a_out = chunk_spatial_gate(u.reshape(b, l, A_HEADS, A_HEAD_DIM),
                               v.reshape(b, l, A_HEADS, A_HEAD_DIM), w_s, b_s).reshape(b, l, A_WIDTH)
    z_ext = z if hist is None else jnp.concatenate([hist.astype(z.dtype), z], axis=1)
    pooled = multiscale_pool(z_ext.reshape(b, -1, B_GROUPS, B_GROUP_DIM), l)
    b_out = jnp.einsum('blgc,gcd->blgd', pooled, w_pool) * pool_scale
    mixed = jnp.concatenate([a_out, b_out.reshape(b, l, B_WIDTH)], axis=-1) @ w_o
    return mixed, z_ext[:, -POOL_HIST:], v


def rope_tables(pos):
    half = QK_ROPE // 2
    inv = 1.0 / (ROPE_THETA ** (jnp.arange(half, dtype=jnp.float32) * 2.0 / QK_ROPE))
    ang = pos.astype(jnp.float32)[:, None] * inv[None, :]
    return jnp.cos(ang), jnp.sin(ang)


def apply_rope(x, cos, sin):
    half = QK_ROPE // 2
    xf = x.astype(jnp.float32)
    x1, x2 = xf[..., :half], xf[..., half:]
    return jnp.concatenate([x1 * cos - x2 * sin, x1 * sin + x2 * cos], axis=-1).astype(x.dtype)


def mla_project(h, pos, w_dqkv, g_q, g_kv, w_uq):
    b, l, _ = h.shape
    d = h @ w_dqkv
    c_q = rmsnorm(d[..., :Q_LORA], g_q)
    c_kv = rmsnorm(d[..., Q_LORA:Q_LORA + KV_LORA], g_kv)
    cos, sin = rope_tables(pos)
    k_pe = apply_rope(d[..., Q_LORA + KV_LORA:], cos, sin)
    q = (c_q @ w_uq).reshape(b, l, N_HEADS, QK_NOPE + QK_ROPE)
    q_nope = q[..., :QK_NOPE]
    q_pe = apply_rope(q[..., QK_NOPE:], cos[:, None, :], sin[:, None, :])
    latent = jnp.concatenate([c_kv, k_pe], axis=-1)
    return q_nope, q_pe, latent


def mla_prompt(q_nope, q_pe, latent, w_uk, w_uv):
    b, l = q_nope.shape[:2]
    c_kv, k_pe = latent[..., :KV_LORA], latent[..., KV_LORA:]
    k_nope = jnp.einsum('bsc,chd->bshd', c_kv, w_uk)
    v = jnp.einsum('bsc,chd->bshd', c_kv, w_uv)
    n_blk = l // Q_BLOCK
    qn = jnp.moveaxis(q_nope.reshape(b, n_blk, Q_BLOCK, N_HEADS, QK_NOPE), 1, 0)
    qp = jnp.moveaxis(q_pe.reshape(b, n_blk, Q_BLOCK, N_HEADS, QK_ROPE), 1, 0)
    kpos = jnp.arange(l)

    def block(args):
        qn_b, qp_b, i = args
        s = (jnp.einsum('bqhd,bkhd->bhqk', qn_b, k_nope)
             + jnp.einsum('bqhr,bkr->bhqk', qp_b, k_pe)).astype(jnp.float32) * ATTN_SCALE
        qpos = i * Q_BLOCK + jnp.arange(Q_BLOCK)
        s = jnp.where(kpos[None, :] <= qpos[:, None], s, -jnp.inf)
        p = jax.nn.softmax(s, axis=-1).astype(v.dtype)
        return jnp.einsum('bhqk,bkhd->bqhd', p, v)

    o = lax.map(block, (qn, qp, jnp.arange(n_blk)))
    return jnp.moveaxis(o, 0, 1).reshape(b, l, N_HEADS * V_HEAD)


def mla_sample(q_nope, q_pe, lat_new, cache, page_table, w_uk, w_uv):
    db, t = q_nope.shape[:2]
    past = cache[page_table].reshape(db, -1, LATENT).astype(lat_new.dtype)
    n_past = past.shape[1]
    q_lat = jnp.einsum('bthd,chd->bthc', q_nope, w_uk)

    def scores(keys):
        return (jnp.einsum('bthc,bkc->bhtk', q_lat, keys[..., :KV_LORA])
                + jnp.einsum('bthr,bkr->bhtk', q_pe, keys[..., KV_LORA:])).astype(jnp.float32) * ATTN_SCALE

    s_past = scores(past)
    s_new = jnp.where(jnp.tril(jnp.ones((t, t), dtype=bool)), scores(lat_new), -jnp.inf)
    p = jax.nn.softmax(jnp.concatenate([s_past, s_new], axis=-1), axis=-1).astype(lat_new.dtype)
    o_lat = (jnp.einsum('bhtk,bkc->bthc', p[..., :n_past], past[..., :KV_LORA])
             + jnp.einsum('bhtk,bkc->bthc', p[..., n_past:], lat_new[..., :KV_LORA]))
    return jnp.einsum('bthc,chd->bthd', o_lat, w_uv).reshape(db, t, N_HEADS * V_HEAD)


def moe_swiglu(h, w_router, w_gate, w_up, w_down):
    logits = (h @ w_router).astype(jnp.float32)
    top_val, top_idx = lax.top_k(logits, TOP_K)
    gates = jax.nn.softmax(top_val, axis=-1)
    combine = jnp.einsum('...k,...ke->...e', gates, jax.nn.one_hot(top_idx, N_EXPERTS, dtype=jnp.float32))
    out = jnp.zeros_like(h)
    for e in range(N_EXPERTS):
        out = out + combine[..., e:e + 1].astype(h.dtype) * swiglu(h, w_gate[e], w_up[e], w_down[e])
    return out


def setup_inputs(seed: int = 0) -> dict:
    key = jax.random.key(seed)
    ks = iter(jax.random.split(key, 40))
    f32 = jnp.float32

    def nrm(shape, scale=1.0):
        return jax.random.normal(next(ks), shape, f32) * scale

    def gain(shape):
        return 1.0 + 0.02 * jax.random.normal(next(ks), shape, f32)

    n_pages = PAST_LEN // PAGE_SIZE
    pages_used = DEC_BATCH * n_pages
    n_pool = pages_used + pages_used // 4
    x_prompt = nrm((BATCH, SEQ, D_MODEL))
    x_sample = nrm((DEC_BATCH, DEC_SEQ, D_MODEL))
    state_pool = nrm((N_EVEN, DEC_BATCH, POOL_HIST, B_WIDTH))
    cache_mla = nrm((N_ODD, n_pool, PAGE_SIZE, LATENT))
    page_table = jax.random.permutation(next(ks), n_pool)[:pages_used].reshape(DEC_BATCH, n_pages).astype(jnp.int32)
    return {
        "x_prompt": x_prompt,
        "x_sample": x_sample,
        "state_pool": state_pool,
        "cache_mla": cache_mla,
        "page_table": page_table,
        "norm_mix": gain((DEPTH, D_MODEL)),
        "norm_ffn": gain((DEPTH, D_MODEL)),
        "w_in": nrm((N_EVEN, D_MODEL, IN_WIDTH), D_MODEL ** -0.5),
        "g_v": gain((N_EVEN, A_WIDTH)),
        "w_s": nrm((N_EVEN, A_HEADS, CHUNK, CHUNK), CHUNK ** -0.5),
        "b_s": gain((N_EVEN, A_HEADS, CHUNK)),
        "w_pool": nrm((N_EVEN, B_GROUPS, B_GROUP_DIM, B_GROUP_DIM), B_GROUP_DIM ** -0.5),
        "pool_scale": gain((N_EVEN, B_GROUPS, B_GROUP_DIM)),
        "w_o_mix": nrm((N_EVEN, MIX_WIDTH, D_MODEL), MIX_WIDTH ** -0.5),
        "w_ffn_gate": nrm((N_EVEN, D_MODEL, D_FF), D_MODEL ** -0.5),
        "w_ffn_up": nrm((N_EVEN, D_MODEL, D_FF), D_MODEL ** -0.5),
        "w_ffn_down": nrm((N_EVEN, D_FF, D_MODEL), D_FF ** -0.5),
        "w_dqkv": nrm((N_ODD, D_MODEL, Q_LORA + LATENT), D_MODEL ** -0.5),
        "g_q": gain((N_ODD, Q_LORA)),
        "g_kv": gain((N_ODD, KV_LORA)),
        "w_uq": nrm((N_ODD, Q_LORA, N_HEADS * (QK_NOPE + QK_ROPE)), Q_LORA ** -0.5),
        "w_uk": nrm((N_ODD, KV_LORA, N_HEADS, QK_NOPE), KV_LORA ** -0.5),
        "w_uv": nrm((N_ODD, KV_LORA, N_HEADS, V_HEAD), KV_LORA ** -0.5),
        "w_o_attn": nrm((N_ODD, N_HEADS * V_HEAD, D_MODEL), (N_HEADS * V_HEAD) ** -0.5),
        "w_router": nrm((N_ODD, D_MODEL, N_EXPERTS), D_MODEL ** -0.5),
        "w_exp_gate": nrm((N_ODD, N_EXPERTS, D_MODEL, D_FF), D_MODEL ** -0.5),
        "w_exp_up": nrm((N_ODD, N_EXPERTS, D_MODEL, D_FF), D_MODEL ** -0.5),
        "w_exp_down": nrm((N_ODD, N_EXPERTS, D_FF, D_MODEL), D_FF ** -0.5),
        "norm_final": gain((D_MODEL,)),
    }


def reference(x_prompt, x_sample, state_pool, cache_mla, page_table,
              norm_mix, norm_ffn, w_in, g_v, w_s, b_s, w_pool, pool_scale, w_o_mix,
              w_ffn_gate, w_ffn_up, w_ffn_down,
              w_dqkv, g_q, g_kv, w_uq, w_uk, w_uv, w_o_attn,
              w_router, w_exp_gate, w_exp_up, w_exp_down, norm_final):
    pos_p = jnp.arange(x_prompt.shape[1])
    pos_s = PAST_LEN + jnp.arange(x_sample.shape[1])
    hp, hs = x_prompt, x_sample
    pool_p, pool_s, chunk_v_s, mla_p, mla_s = [], [], [], [], []
    for layer in range(DEPTH):
        i = layer // 2
        if layer % 2 == 0:
            mp, hist_p, _ = even_mixer(rmsnorm(hp, norm_mix[layer]), None, w_in[i], g_v[i], w_s[i], b_s[i],
                                       w_pool[i], pool_scale[i], w_o_mix[i])
            ms, hist_s, v_s = even_mixer(rmsnorm(hs, norm_mix[layer]), state_pool[i], w_in[i], g_v[i], w_s[i], b_s[i],
                                         w_pool[i], pool_scale[i], w_o_mix[i])
            hp = hp + mp
            hs = hs + ms
            pool_p.append(hist_p)
            pool_s.append(hist_s)
            chunk_v_s.append(v_s)
            hp = hp + swiglu(rmsnorm(hp, norm_ffn[layer]), w_ffn_gate[i], w_ffn_up[i], w_ffn_down[i])
            hs = hs + swiglu(rmsnorm(hs, norm_ffn[layer]), w_ffn_gate[i], w_ffn_up[i], w_ffn_down[i])
        else:
            qn_p, qp_p, lat_p = mla_project(rmsnorm(hp, norm_mix[layer]), pos_p, w_dqkv[i], g_q[i], g_kv[i], w_uq[i])
            qn_s, qp_s, lat_s = mla_project(rmsnorm(hs, norm_mix[layer]), pos_s, w_dqkv[i], g_q[i], g_kv[i], w_uq[i])
            hp = hp + mla_prompt(qn_p, qp_p, lat_p, w_uk[i], w_uv[i]) @ w_o_attn[i]
            hs = hs + mla_sample(qn_s, qp_s, lat_s, cache_mla[i], page_table, w_uk[i], w_uv[i]) @ w_o_attn[i]
            mla_p.append(lat_p)
            mla_s.append(lat_s)
            hp = hp + moe_swiglu(rmsnorm(hp, norm_ffn[layer]), w_router[i], w_exp_gate[i], w_exp_up[i], w_exp_down[i])
            hs = hs + moe_swiglu(rmsnorm(hs, norm_ffn[layer]), w_router[i], w_exp_gate[i], w_exp_up[i], w_exp_down[i])
    y_prompt = rmsnorm(hp, norm_final)
    y_sample = rmsnorm(hs, norm_final)
    pool_state_prompt = jnp.stack(pool_p)
    pool_state_sample = jnp.stack(pool_s)
    chunk_v_sample = jnp.stack(chunk_v_s)
    mla_rows_prompt = jnp.stack(mla_p)
    mla_rows_sample = jnp.stack(mla_s)
    return (y_prompt, y_sample, pool_state_prompt, pool_state_sample, chunk_v_sample, mla_rows_prompt, mla_rows_sample)
```

```python
import functools

import numpy as np
import jax
import jax.numpy as jnp
from jax import lax
from jax.experimental import pallas as pl
from jax.experimental.pallas import tpu as pltpu

D_MODEL = 2048
BATCH = 4
SEQ = 2048
DEC_BATCH = 128
PAST_LEN = 8192
PAGE_SIZE = 128
N_PAGES = PAST_LEN // PAGE_SIZE
A_WIDTH = 1024
A_HEADS = 8
A_HEAD_DIM = 128
CHUNK = 128
B_WIDTH = 1024
POOL_WINDOWS = (2, 4, 8, 16)
B_GROUP_DIM = 256
POOL_HIST = 15
N_HEADS = 16
QK_NOPE = 128
QK_ROPE = 64
V_HEAD = 128
Q_LORA = 512
KV_LORA = 512
LATENT = KV_LORA + QK_ROPE
ROPE_THETA = 10000.0
ATTN_SCALE = (QK_NOPE + QK_ROPE) ** -0.5
D_FF = 7168
N_EXPERTS = 8
TOP_K = 2
EPS = 1e-6

T_P = BATCH * SEQ
T_S = DEC_BATCH
T = T_P + T_S

BF = jnp.bfloat16
F32 = jnp.float32

VMEM_LIMIT_BYTES = 56 * 1024 * 1024

TM = 640
N_TM = T // TM
TM_FFN = 1040
TF = 256
HEAD_PAD = 256
TQ = 512
N_QT = SEQ // TQ
PAGES_PER_STEP = 16
N_KV_STEPS = N_PAGES // PAGES_PER_STEP
MOE_TM = 1024
MOE_SUB = 256
MOE_NT = (T * TOP_K) // MOE_TM + N_EXPERTS
MOE_ROWS = MOE_NT * MOE_TM
TC = 128
N_TC = T // TC
N_TC_P = T_P // TC


def _cparams(sem):
    return pltpu.CompilerParams(dimension_semantics=sem, vmem_limit_bytes=VMEM_LIMIT_BYTES)


def _rms(x, g):
    return x * lax.rsqrt(jnp.mean(x * x, axis=-1, keepdims=True) + EPS) * g


def _gelu(x):
    return 0.5 * x * (1.0 + lax.erf(x * np.float32(np.sqrt(0.5))))


def _bdot(a, b):
    return jnp.dot(a, b, preferred_element_type=F32)


def _bdot_nt(a, b):
    return lax.dot_general(a, b, (((1,), (1,)), ((), ())), preferred_element_type=F32)


def _rope128(x, c, s):
    swapped = pltpu.roll(x, 96, axis=1) + pltpu.roll(x, 32, axis=1)
    return x * c + swapped * s


def _inproj_body(x_ref, g_ref, w_ref, gv_ref, o_ref):
    j = pl.program_id(0)
    xn = _rms(x_ref[...], g_ref[...]).astype(BF)
    y = _bdot(xn, w_ref[...].astype(BF))

    @pl.when(j == 0)
    def _():
        o_ref[0] = _gelu(y)

    @pl.when(j == 1)
    def _():
        o_ref[0] = _rms(_gelu(y), gv_ref[...])

    @pl.when(j == 2)
    def _():
        o_ref[0] = y


def _inproj(x, g, w, gv):
    return pl.pallas_call(
        _inproj_body,
        grid=(3, N_TM),
        in_specs=[
            pl.BlockSpec((TM, D_MODEL), lambda j, i: (i, 0)),
            pl.BlockSpec((1, D_MODEL), lambda j, i: (0, 0)),
            pl.BlockSpec((D_MODEL, A_WIDTH), lambda j, i: (0, j)),
            pl.BlockSpec((1, A_WIDTH), lambda j, i: (0, 0)),
        ],
        out_specs=pl.BlockSpec((1, TM, A_WIDTH), lambda j, i: (j, i, 0)),
        out_shape=jax.ShapeDtypeStruct((3, T, A_WIDTH), F32),
        compiler_params=_cparams(("arbitrary", "arbitrary")),
        name="inproj",
    )(x, g, w, gv)


def _gate_body(u_ref, v_ref, ws_ref, bb_ref, a_ref):
    is_sample = pl.program_id(0) >= N_TC_P
    row = lax.broadcasted_iota(jnp.int32, (CHUNK, CHUNK), 0)
    col = lax.broadcasted_iota(jnp.int32, (CHUNK, CHUNK), 1)
    for h in range(A_HEADS):
        w = ws_ref[h]
        w_chunk = jnp.where(col <= row, w, 0.0)
        w_single = jnp.where(col == row, jnp.broadcast_to(w[0:1, 0:1], (CHUNK, CHUNK)), 0.0)
        w_eff = jnp.where(is_sample, w_single, w_chunk).astype(BF)
        b = bb_ref[h]
        b_eff = jnp.where(is_sample, jnp.broadcast_to(b[0:1, :], (CHUNK, CHUNK)), b)
        sl = slice(h * A_HEAD_DIM, (h + 1) * A_HEAD_DIM)
        mixed = _bdot(w_eff, v_ref[0, :, sl].astype(BF)) + b_eff
        a_ref[:, sl] = (u_ref[0, :, sl] * mixed).astype(BF)


def _gate(y, w_s, b_bcast):
    return pl.pallas_call(
        _gate_body,
        grid=(N_TC,),
        in_specs=[
            pl.BlockSpec((1, TC, A_WIDTH), lambda i: (0, i, 0)),
            pl.BlockSpec((1, TC, A_WIDTH), lambda i: (1, i, 0)),
            pl.BlockSpec((A_HEADS, CHUNK, CHUNK), lambda i: (0, 0, 0)),
            pl.BlockSpec((A_HEADS, CHUNK, CHUNK), lambda i: (0, 0, 0)),
        ],
        out_specs=pl.BlockSpec((TC, A_WIDTH), lambda i: (i, 0)),
        out_shape=jax.ShapeDtypeStruct((T, A_WIDTH), BF),
        compiler_params=_cparams(("arbitrary",)),
        name="gate",
    )(y, y, w_s, b_bcast)


def _pool_body(z_ref, hist_ref, wp_ref, ps_ref, o_ref):
    b = pl.program_id(0)

    def project(pooled, g):
        out = _bdot(pooled.astype(BF), wp_ref[g].astype(BF)) * ps_ref[g:g + 1, :]
        return out.astype(BF)

    @pl.when(b < BATCH)
    def _():
        row = lax.broadcasted_iota(jnp.int32, (SEQ, B_GROUP_DIM), 0)
        for g, w in enumerate(POOL_WINDOWS):
            sl = slice(g * B_GROUP_DIM, (g + 1) * B_GROUP_DIM)
            z = z_ref[0, :, sl]
            s = z
            k = 1
            while k < w:
                s = s + jnp.where(row >= k, pltpu.roll(s, k, axis=0), 0.0)
                k *= 2
            cnt = jnp.minimum(row + 1, w).astype(F32)
            o_ref[:, sl] = project(s / cnt - z, g)

    @pl.when(b == BATCH)
    def _():
        for g, w in enumerate(POOL_WINDOWS):
            sl = slice(g * B_GROUP_DIM, (g + 1) * B_GROUP_DIM)
            z = z_ref[0, 0:T_S, sl]
            s = z
            for k in range(POOL_HIST + 1 - w, POOL_HIST):
                s = s + hist_ref[:, k * B_WIDTH + g * B_GROUP_DIM:k * B_WIDTH + (g + 1) * B_GROUP_DIM]
            o_ref[0:T_S, sl] = project(s / np.float32(w) - z, g)


def _pool(y, hist2d, w_pool, pool_scale):
    return pl.pallas_call(
        _pool_body,
        grid=(BATCH + 1,),
        in_specs=[
            pl.BlockSpec((1, SEQ, B_WIDTH), lambda b: (2, b, 0)),
            pl.BlockSpec((T_S, POOL_HIST * B_WIDTH), lambda b: (0, 0)),
            pl.BlockSpec((len(POOL_WINDOWS), B_GROUP_DIM, B_GROUP_DIM), lambda b: (0, 0, 0)),
            pl.BlockSpec((len(POOL_WINDOWS), B_GROUP_DIM), lambda b: (0, 0)),
        ],
        out_specs=pl.BlockSpec((SEQ, B_WIDTH), lambda b: (b, 0)),
        out_shape=jax.ShapeDtypeStruct((T, B_WIDTH), BF),
        compiler_params=_cparams(("arbitrary",)),
        name="pool",
    )(y, hist2d, w_pool, pool_scale)


def _proj_res_body(*refs, n_in):
    a_refs = refs[:n_in]
    w_ref, r_ref, o_ref = refs[n_in:]
    acc = r_ref[...]
    off = 0
    for a_ref in a_refs:
        k = a_ref.shape[1]
        acc = acc + _bdot(a_ref[...], w_ref[off:off + k, :].astype(BF))
        off += k
    o_ref[...] = acc


def _proj_res(acts, w, res, name):
    tn = 1024
    k_total = w.shape[0]
    return pl.pallas_call(
        functools.partial(_proj_res_body, n_in=len(acts)),
        grid=(D_MODEL // tn, N_TM),
        in_specs=[pl.BlockSpec((TM, a.shape[1]), lambda j, i: (i, 0)) for a in acts] + [
            pl.BlockSpec((k_total, tn), lambda j, i: (0, j)),
            pl.BlockSpec((TM, tn), lambda j, i: (i, j)),
        ],
        out_specs=pl.BlockSpec((TM, tn), lambda j, i: (i, j)),
        out_shape=jax.ShapeDtypeStruct((T, D_MODEL), F32),
        compiler_params=_cparams(("arbitrary", "arbitrary")),
        name=name,
    )(*acts, w, res)


def _ffn_body(x_hbm, g_ref, wg_ref, wu_ref, wd_ref, o_ref, xn_ref, sem):
    i = pl.program_id(0)
    f = pl.program_id(1)

    @pl.when(f == 0)
    def _():
        cp = pltpu.make_async_copy(x_hbm.at[pl.ds(i * TM_FFN, TM_FFN)], o_ref, sem)
        cp.start()
        cp.wait()
        xn_ref[...] = _rms(o_ref[...], g_ref[...]).astype(BF)

    xn = xn_ref[...]
    gate = _bdot(xn, wg_ref[...].astype(BF))
    up = _bdot(xn, wu_ref[...].astype(BF))
    act = (gate * jax.nn.sigmoid(gate) * up).astype(BF)
    o_ref[...] += _bdot(act, wd_ref[...].astype(BF))


def _ffn(x, g, wg, wu, wd):
    return pl.pallas_call(
        _ffn_body,
        grid=(T // TM_FFN, D_FF // TF),
        in_specs=[
            pl.BlockSpec(memory_space=pl.ANY),
            pl.BlockSpec((1, D_MODEL), lambda i, f: (0, 0)),
            pl.BlockSpec((D_MODEL, TF), lambda i, f: (0, f)),
            pl.BlockSpec((D_MODEL, TF), lambda i, f: (0, f)),
            pl.BlockSpec((TF, D_MODEL), lambda i, f: (f, 0)),
        ],
        out_specs=pl.BlockSpec((TM_FFN, D_MODEL), lambda i, f: (i, 0)),
        out_shape=jax.ShapeDtypeStruct((T, D_MODEL), F32),
        scratch_shapes=[pltpu.VMEM((TM_FFN, D_MODEL), BF), pltpu.SemaphoreType.DMA(())],
        compiler_params=_cparams(("arbitrary", "arbitrary")),
        name="ffn",
    )(x, g, wg, wu, wd)


def _dqkv_body(x_ref, g_ref, w_ref, gq_ref, gkv_ref, cos_ref, sin_ref,
               cq_ref, ckv_ref, pe_ref, lat_ref):
    xn = _rms(x_ref[...], g_ref[...]).astype(BF)
    d = _bdot(xn, w_ref[...].astype(BF))
    cq_ref[...] = _rms(d[:, :Q_LORA], gq_ref[...]).astype(BF)
    ckv = _rms(d[:, Q_LORA:Q_LORA + KV_LORA], gkv_ref[...])
    pe = _rope128(d[:, Q_LORA + KV_LORA:], cos_ref[...], sin_ref[...])
    ckv_ref[...] = ckv.astype(BF)
    pe_ref[...] = pe.astype(BF)
    lat_ref[:, :KV_LORA] = ckv
    lat_ref[:, KV_LORA:] = pe[:, :QK_ROPE]


def _dqkv(x, g, w_pad, gq, gkv, cos_t, sin_t):
    n = w_pad.shape[1]
    row = lambda i: (i, 0)
    fix = lambda i: (0, 0)
    return pl.pallas_call(
        _dqkv_body,
        grid=(N_TM,),
        in_specs=[
            pl.BlockSpec((TM, D_MODEL), row),
            pl.BlockSpec((1, D_MODEL), fix),
            pl.BlockSpec((D_MODEL, n), fix),
            pl.BlockSpec((1, Q_LORA), fix),
            pl.BlockSpec((1, KV_LORA), fix),
            pl.BlockSpec((TM, 128), row),
            pl.BlockSpec((TM, 128), row),
        ],
        out_specs=[
            pl.BlockSpec((TM, Q_LORA), row),
            pl.BlockSpec((TM, KV_LORA), row),
            pl.BlockSpec((TM, 128), row),
            pl.BlockSpec((TM, LATENT), row),
        ],
        out_shape=[
            jax.ShapeDtypeStruct((T, Q_LORA), BF),
            jax.ShapeDtypeStruct((T, KV_LORA), BF),
            jax.ShapeDtypeStruct((T, 128), BF),
            jax.ShapeDtypeStruct((T, LATENT), F32),
        ],
        compiler_params=_cparams(("arbitrary",)),
        name="dqkv",
    )(x, g, w_pad, gq, gkv, cos_t, sin_t)


HG = 4


def _qkv_body(cq_ref, ckv_ref, pe_ref, cos_ref, sin_ref, wq_ref, wk_ref, wv_ref,
              q_ref, k_ref, v_ref):
    q = _bdot(cq_ref[...], wq_ref[...].astype(BF)) * np.float32(ATTN_SCALE)
    ckv = ckv_ref[...]
    kn = _bdot(ckv, wk_ref[...].astype(BF))
    pe = pe_ref[...]
    c = cos_ref[...]
    s = sin_ref[...]
    for h in range(HG):
        o = h * HEAD_PAD
        q_ref[:, o:o + QK_NOPE] = q[:, o:o + QK_NOPE].astype(BF)
        q_ref[:, o + QK_NOPE:o + HEAD_PAD] = _rope128(q[:, o + QK_NOPE:o + HEAD_PAD], c, s).astype(BF)
        k_ref[:, o:o + QK_NOPE] = kn[:, h * QK_NOPE:(h + 1) * QK_NOPE].astype(BF)
        k_ref[:, o + QK_NOPE:o + HEAD_PAD] = pe
    v_ref[...] = _bdot(ckv, wv_ref[...].astype(BF)).astype(BF)


def _qkv(cq, ckv, pe, cos_t, sin_t, wq_pad, wk, wv):
    row = lambda j, i: (i, 0)
    col = lambda j, i: (0, j)
    out = lambda j, i: (i, j)
    return pl.pallas_call(
        _qkv_body,
        grid=(N_HEADS // HG, N_TM),
        in_specs=[
            pl.BlockSpec((TM, Q_LORA), row),
            pl.BlockSpec((TM, KV_LORA), row),
            pl.BlockSpec((TM, 128), row),
            pl.BlockSpec((TM, 128), row),
            pl.BlockSpec((TM, 128), row),
            pl.BlockSpec((Q_LORA, HG * HEAD_PAD), col),
            pl.BlockSpec((KV_LORA, HG * QK_NOPE), col),
            pl.BlockSpec((KV_LORA, HG * V_HEAD), col),
        ],
        out_specs=[
            pl.BlockSpec((TM, HG * HEAD_PAD), out),
            pl.BlockSpec((TM, HG * HEAD_PAD), out),
            pl.BlockSpec((TM, HG * V_HEAD), out),
        ],
        out_shape=[
            jax.ShapeDtypeStruct((T, N_HEADS * HEAD_PAD), BF),
            jax.ShapeDtypeStruct((T, N_HEADS * HEAD_PAD), BF),
            jax.ShapeDtypeStruct((T, N_HEADS * V_HEAD), BF),
        ],
        compiler_params=_cparams(("arbitrary", "arbitrary")),
        name="qkv",
    )(cq, ckv, pe, cos_t, sin_t, wq_pad, wk, wv)


_PAIRS = [(qi, ki) for qi in range(N_QT) for ki in range(qi + 1)]


def _flash_body(qi_tab, ki_tab, q_ref, k_ref, v_ref, o_ref, m_ref, l_ref, acc_ref):
    p = pl.program_id(2)
    qi = qi_tab[p]
    ki = ki_tab[p]

    @pl.when(ki == 0)
    def _():
        m_ref[...] = jnp.full(m_ref.shape, -jnp.inf, F32)
        l_ref[...] = jnp.zeros(l_ref.shape, F32)
        acc_ref[...] = jnp.zeros(acc_ref.shape, F32)

    s = _bdot_nt(q_ref[...], k_ref[...])
    row = lax.broadcasted_iota(jnp.int32, (TQ, TQ), 0)
    col = lax.broadcasted_iota(jnp.int32, (TQ, TQ), 1)
    s = jnp.where(col - row <= (qi - ki) * TQ, s, -jnp.inf)
    m_prev = m_ref[...]
    m_new = jnp.maximum(m_prev, jnp.max(s, axis=-1, keepdims=True))
    alpha = jnp.exp(m_prev - m_new)
    pexp = jnp.exp(s - m_new)
    l_ref[...] = alpha * l_ref[...] + jnp.sum(pexp, axis=-1, keepdims=True)
    acc_ref[...] = alpha * acc_ref[...] + _bdot(pexp.astype(BF), v_ref[...])
    m_ref[...] = m_new

    @pl.when(ki == qi)
    def _():
        o_ref[...] = (acc_ref[...] / l_ref[...]).astype(BF)


def _flash(q, k, v):
    qi_tab = jnp.asarray(np.array([p[0] for p in _PAIRS], np.int32))
    ki_tab = jnp.asarray(np.array([p[1] for p in _PAIRS], np.int32))
    grid_spec = pltpu.PrefetchScalarGridSpec(
        num_scalar_prefetch=2,
        grid=(BATCH, N_HEADS, len(_PAIRS)),
        in_specs=[
            pl.BlockSpec((TQ, HEAD_PAD), lambda b, h, p, qt, kt: (b * N_QT + qt[p], h)),
            pl.BlockSpec((TQ, HEAD_PAD), lambda b, h, p, qt, kt: (b * N_QT + kt[p], h)),
            pl.BlockSpec((TQ, V_HEAD), lambda b, h, p, qt, kt: (b * N_QT + kt[p], h)),
        ],
        out_specs=pl.BlockSpec((TQ, V_HEAD), lambda b, h, p, qt, kt: (b * N_QT + qt[p], h)),
        scratch_shapes=[
            pltpu.VMEM((TQ, 1), F32),
            pltpu.VMEM((TQ, 1), F32),
            pltpu.VMEM((TQ, V_HEAD), F32),
        ],
    )
    return pl.pallas_call(
        _flash_body,
        grid_spec=grid_spec,
        out_shape=jax.ShapeDtypeStruct((T_P, N_HEADS * V_HEAD), BF),
        compiler_params=_cparams(("arbitrary", "arbitrary", "arbitrary")),
        name="flash",
    )(qi_tab, ki_tab, q, k, v)


def _qlat_body(q_ref, wk_ref, o_ref):
    o_ref[...] = _bdot_nt(q_ref[:, :QK_NOPE], wk_ref[...].astype(BF)).astype(BF)


def _qlat(q, wk):
    return pl.pallas_call(
        _qlat_body,
        grid=(N_HEADS,),
        in_specs=[
            pl.BlockSpec((T_S, HEAD_PAD), lambda h: (T_P // T_S, h)),
            pl.BlockSpec((KV_LORA, QK_NOPE), lambda h: (0, h)),
        ],
        out_specs=pl.BlockSpec((T_S, KV_LORA), lambda h: (0, h)),
        out_shape=jax.ShapeDtypeStruct((T_S, N_HEADS * KV_LORA), BF),
        compiler_params=_cparams(("arbitrary",)),
        name="qlat",
    )(q, wk)


def _decode_body(pt_ref, qlat_ref, q_ref, lat_ref, cache_hbm, o_ref,
                 buf, sems, m_ref, l_ref, acc_ref):
    b = pl.program_id(0)
    c = pl.program_id(1)
    step = b * N_KV_STEPS + c
    n_steps = DEC_BATCH * N_KV_STEPS
    slot = step % 2

    def page_copy(st, j, sl):
        pg = pt_ref[st * PAGES_PER_STEP + j]
        return pltpu.make_async_copy(cache_hbm.at[pg], buf.at[sl, j], sems.at[sl])

    def start_fetch(st, sl):
        for j in range(PAGES_PER_STEP):
            page_copy(st, j, sl).start()

    @pl.when(step == 0)
    def _():
        start_fetch(0, 0)

    @pl.when(step + 1 < n_steps)
    def _():
        start_fetch(step + 1, 1 - slot)

    qlat = qlat_ref[0]
    qpe = q_ref[0, :, QK_NOPE:QK_NOPE + QK_ROPE]

    @pl.when(c == 0)
    def _():
        lat = lat_ref[0]
        s_self = (jnp.sum(qlat.astype(F32) * lat[:, :KV_LORA], axis=-1, keepdims=True)
                  + jnp.sum(qpe.astype(F32) * lat[:, KV_LORA:], axis=-1, keepdims=True))
        m_ref[...] = s_self
        l_ref[...] = jnp.ones(l_ref.shape, F32)
        acc_ref[...] = jnp.broadcast_to(lat[:, :KV_LORA], acc_ref.shape)

    for j in range(PAGES_PER_STEP):
        page_copy(step, j, slot).wait()

    kv = buf[slot].reshape(PAGES_PER_STEP * PAGE_SIZE, LATENT).astype(BF)
    s = _bdot_nt(qlat, kv[:, :KV_LORA]) + _bdot_nt(qpe, kv[:, KV_LORA:])
    m_prev = m_ref[...]
    m_new = jnp.maximum(m_prev, jnp.max(s, axis=-1, keepdims=True))
    alpha = jnp.exp(m_prev - m_new)
    pexp = jnp.exp(s - m_new)
    l_ref[...] = alpha * l_ref[...] + jnp.sum(pexp, axis=-1, keepdims=True)
    acc_ref[...] = alpha * acc_ref[...] + _bdot(pexp.astype(BF), kv[:, :KV_LORA])
    m_ref[...] = m_new

    @pl.when(c == N_KV_STEPS - 1)
    def _():
        o_ref[0] = acc_ref[...] / l_ref[...]


def _decode(page_table_flat, qlat3, q3, lat3, cache):
    grid_spec = pltpu.PrefetchScalarGridSpec(
        num_scalar_prefetch=1,
        grid=(DEC_BATCH, N_KV_STEPS),
        in_specs=[
            pl.BlockSpec((1, N_HEADS, KV_LORA), lambda b, c, pt: (b, 0, 0)),
            pl.BlockSpec((1, N_HEADS, HEAD_PAD), lambda b, c, pt: (T_P + b, 0, 0)),
            pl.BlockSpec((1, 1, LATENT), lambda b, c, pt: (T_P + b, 0, 0)),
            pl.BlockSpec(memory_space=pl.ANY),
        ],
        out_specs=pl.BlockSpec((1, N_HEADS, KV_LORA), lambda b, c, pt: (b, 0, 0)),
        scratch_shapes=[
            pltpu.VMEM((2, PAGES_PER_STEP, PAGE_SIZE, LATENT), F32),
            pltpu.SemaphoreType.DMA((2,)),
            pltpu.VMEM((N_HEADS, 1), F32),
            pltpu.VMEM((N_HEADS, 1), F32),
            pltpu.VMEM((N_HEADS, KV_LORA), F32),
        ],
    )
    return pl.pallas_call(
        _decode_body,
        grid_spec=grid_spec,
        out_shape=jax.ShapeDtypeStruct((DEC_BATCH, N_HEADS, KV_LORA), F32),
        compiler_params=_cparams(("arbitrary", "arbitrary")),
        name="decode",
    )(page_table_flat, qlat3, q3, lat3, cache)


def _ouv_body(o_ref, wv_ref, out_ref):
    out_ref[...] = _bdot(o_ref[...].astype(BF), wv_ref[...].astype(BF)).astype(BF)


def _ouv(olat2d, wv):
    return pl.pallas_call(
        _ouv_body,
        grid=(N_HEADS,),
        in_specs=[
            pl.BlockSpec((T_S, KV_LORA), lambda h: (0, h)),
            pl.BlockSpec((KV_LORA, V_HEAD), lambda h: (0, h)),
        ],
        out_specs=pl.BlockSpec((T_S, V_HEAD), lambda h: (0, h)),
        out_shape=jax.ShapeDtypeStruct((T_S, N_HEADS * V_HEAD), BF),
        compiler_params=_cparams(("arbitrary",)),
        name="ouv",
    )(olat2d, wv)


def _router_body(x_ref, g_ref, w_ref, meta_ref):
    xn = _rms(x_ref[...], g_ref[...])
    w = w_ref[...]
    xh = xn.astype(BF)
    xl = (xn - xh.astype(F32)).astype(BF)
    wh = w.astype(BF)
    wl = (w - wh.astype(F32)).astype(BF)
    logits = _bdot(xh, wh) + (_bdot(xh, wl) + _bdot(xl, wh))
    lane = lax.broadcasted_iota(jnp.int32, logits.shape, 1).astype(F32)
    lg = jnp.where(lane < N_EXPERTS, logits, -jnp.inf)
    t1 = jnp.max(lg, axis=-1, keepdims=True)
    i1 = jnp.min(jnp.where(lg == t1, lane, 128.0), axis=-1, keepdims=True)
    lg2 = jnp.where(lane == i1, -jnp.inf, lg)
    t2 = jnp.max(lg2, axis=-1, keepdims=True)
    i2 = jnp.min(jnp.where(lg2 == t2, lane, 128.0), axis=-1, keepdims=True)
    e = jnp.exp(t2 - t1)
    g1 = 1.0 / (1.0 + e)
    g2 = e / (1.0 + e)
    meta = jnp.where(lane == 0, i1, jnp.where(lane == 1, i2, jnp.where(lane == 2, g1, g2)))
    meta_ref[...] = meta


def _router(x, g, w_pad):
    return pl.pallas_call(
        _router_body,
        grid=(N_TM,),
        in_specs=[
            pl.BlockSpec((TM, D_MODEL), lambda i: (i, 0)),
            pl.BlockSpec((1, D_MODEL), lambda i: (0, 0)),
            pl.BlockSpec((D_MODEL, 128), lambda i: (0, 0)),
        ],
        out_specs=pl.BlockSpec((TM, 128), lambda i: (i, 0)),
        out_shape=jax.ShapeDtypeStruct((T, 128), F32),
        compiler_params=_cparams(("arbitrary",)),
        name="router",
    )(x, g, w_pad)


N_SUB = MOE_TM // MOE_SUB
N_FF = D_FF // TF


def _moe_body(te_ref, rows_ref, src_ref, h_hbm, g_ref, wg_ref, wu_ref, wd_ref,
              y_ref, xs_ref, xn_ref, sems):
    t = pl.program_id(0)
    f = pl.program_id(1)
    nrows = rows_ref[t]

    @pl.when(f == 0)
    def _():
        for sb in range(N_SUB):
            @pl.when(sb * MOE_SUB < nrows)
            def _():
                def issue(r, carry):
                    tok = src_ref[t * MOE_TM + sb * MOE_SUB + r]
                    pltpu.make_async_copy(h_hbm.at[pl.ds(tok, 1)],
                                          xs_ref.at[pl.ds(sb * MOE_SUB + r, 1)], sems.at[sb]).start()
                    return carry
                lax.fori_loop(0, MOE_SUB, issue, 0)
        for sb in range(N_SUB):
            rs = pl.ds(sb * MOE_SUB, MOE_SUB)

            @pl.when(sb * MOE_SUB < nrows)
            def _():
                pltpu.make_async_copy(h_hbm.at[pl.ds(0, MOE_SUB)], xs_ref.at[rs], sems.at[sb]).wait()
                xn_ref[rs, :] = _rms(xs_ref[rs, :], g_ref[...]).astype(BF)

            @pl.when(sb * MOE_SUB >= nrows)
            def _():
                y_ref[rs, :] = jnp.zeros((MOE_SUB, D_MODEL), F32)

    @pl.when(nrows > 0)
    def _():
        wg = wg_ref[0].astype(BF)
        wu = wu_ref[0].astype(BF)
        wd = wd_ref[0].astype(BF)
        for sb in range(N_SUB):
            rs = pl.ds(sb * MOE_SUB, MOE_SUB)

            @pl.when(sb * MOE_SUB < nrows)
            def _():
                xn = xn_ref[rs, :]
                gate = _bdot(xn, wg)
                up = _bdot(xn, wu)
                act = (gate * jax.nn.sigmoid(gate) * up).astype(BF)
                out = _bdot(act, wd)

                @pl.when(f == 0)
                def _():
                    y_ref[rs, :] = out

                @pl.when(f > 0)
                def _():
                    y_ref[rs, :] += out


def _moe(tile_expert, tile_rows, src, h, g, wg, wu, wd):
    ff = lambda t, f, nr: jnp.where(nr[t] > 0, f, N_FF - 1)
    grid_spec = pltpu.PrefetchScalarGridSpec(
        num_scalar_prefetch=3,
        grid=(MOE_NT, N_FF),
        in_specs=[
            pl.BlockSpec(memory_space=pl.ANY),
            pl.BlockSpec((1, D_MODEL), lambda t, f, te, nr, sr: (0, 0)),
            pl.BlockSpec((1, D_MODEL, TF), lambda t, f, te, nr, sr: (te[t], 0, ff(t, f, nr))),
            pl.BlockSpec((1, D_MODEL, TF), lambda t, f, te, nr, sr: (te[t], 0, ff(t, f, nr))),
            pl.BlockSpec((1, TF, D_MODEL), lambda t, f, te, nr, sr: (te[t], ff(t, f, nr), 0)),
        ],
        out_specs=pl.BlockSpec((MOE_TM, D_MODEL), lambda t, f, te, nr, sr: (t, 0)),
        scratch_shapes=[
            pltpu.VMEM((MOE_TM, D_MODEL), F32),
            pltpu.VMEM((MOE_TM, D_MODEL), BF),
            pltpu.SemaphoreType.DMA((N_SUB,)),
        ],
    )
    return pl.pallas_call(
        _moe_body,
        grid_spec=grid_spec,
        out_shape=jax.ShapeDtypeStruct((MOE_ROWS, D_MODEL), F32),
        compiler_params=_cparams(("arbitrary", "arbitrary")),
        name="moe",
    )(tile_expert, tile_rows, src, h, g, wg, wu, wd)


def _combine_body(pos_ref, h_ref, meta_ref, gf_ref, y_hbm, yp_ref, ys_ref, yb_ref, sem):
    i = pl.program_id(0)

    def issue(r, carry):
        tok = i * TC + r
        for k in range(TOP_K):
            pltpu.make_async_copy(y_hbm.at[pl.ds(pos_ref[TOP_K * tok + k], 1)],
                                  yb_ref.at[k, pl.ds(r, 1)], sem).start()
        return carry

    lax.fori_loop(0, TC, issue, 0)
    for k in range(TOP_K):
        pltpu.make_async_copy(y_hbm.at[pl.ds(0, TC)], yb_ref.at[k], sem).wait()
    meta = meta_ref[...]
    out = h_ref[...] + meta[:, 2:3] * yb_ref[0] + meta[:, 3:4] * yb_ref[1]
    out = _rms(out, gf_ref[...])

    @pl.when(i < N_TC_P)
    def _():
        yp_ref[...] = out

    @pl.when(i >= N_TC_P)
    def _():
        ys_ref[...] = out


def _combine(pos, h, meta, gf, y):
    grid_spec = pltpu.PrefetchScalarGridSpec(
        num_scalar_prefetch=1,
        grid=(N_TC,),
        in_specs=[
            pl.BlockSpec((TC, D_MODEL), lambda i, pos: (i, 0)),
            pl.BlockSpec((TC, 128), lambda i, pos: (i, 0)),
            pl.BlockSpec((1, D_MODEL), lambda i, pos: (0, 0)),
            pl.BlockSpec(memory_space=pl.ANY),
        ],
        out_specs=[
            pl.BlockSpec((TC, D_MODEL), lambda i, pos: (jnp.minimum(i, N_TC_P - 1), 0)),
            pl.BlockSpec((T_S, D_MODEL), lambda i, pos: (0, 0)),
        ],
        scratch_shapes=[
            pltpu.VMEM((TOP_K, TC, D_MODEL), F32),
            pltpu.SemaphoreType.DMA(()),
        ],
    )
    return pl.pallas_call(
        _combine_body,
        grid_spec=grid_spec,
        out_shape=[
            jax.ShapeDtypeStruct((T_P, D_MODEL), F32),
            jax.ShapeDtypeStruct((T_S, D_MODEL), F32),
        ],
        compiler_params=_cparams(("arbitrary",)),
        name="combine",
    )(pos, h, meta, gf, y)


def _routing_tables(experts):
    flat = experts.reshape(-1)
    onehot = (flat[:, None] == jnp.arange(N_EXPERTS, dtype=jnp.int32)[None, :]).astype(jnp.int32)
    csum = jnp.cumsum(onehot, axis=0)
    rank = jnp.take_along_axis(csum, flat[:, None], axis=1)[:, 0] - 1
    counts = csum[-1]
    tiles_per = (counts + MOE_TM - 1) // MOE_TM
    tile_end = jnp.cumsum(tiles_per)
    tile_start = tile_end - tiles_per
    n_used = tile_end[-1]
    pos = tile_start[flat] * MOE_TM + rank
    tid = jnp.arange(MOE_NT, dtype=jnp.int32)
    last_tile = jnp.maximum(n_used - 1, 0)
    te = jnp.searchsorted(tile_end, jnp.minimum(tid, last_tile), side="right").astype(jnp.int32)
    te = jnp.minimum(te, N_EXPERTS - 1)
    rows = jnp.clip(counts[te] - (tid - tile_start[te]) * MOE_TM, 0, MOE_TM)
    rows = jnp.where(tid < n_used, rows, 0).astype(jnp.int32)
    tok = jnp.arange(T * TOP_K, dtype=jnp.int32) // TOP_K
    src = jnp.zeros((MOE_ROWS,), jnp.int32).at[pos].set(tok)
    return te, rows, src, pos.astype(jnp.int32)


def _rope_tables():
    half = QK_ROPE // 2
    inv = 1.0 / (ROPE_THETA ** (jnp.arange(half, dtype=F32) * 2.0 / QK_ROPE))
    pos = jnp.concatenate([jnp.tile(jnp.arange(SEQ), BATCH), jnp.full((T_S,), PAST_LEN)])
    ang = pos.astype(F32)[:, None] * inv[None, :]
    cos, sin = jnp.cos(ang), jnp.sin(ang)
    zero = jnp.zeros((T, 128 - QK_ROPE), F32)
    return (jnp.concatenate([cos, cos, zero], axis=1),
            jnp.concatenate([-sin, sin, zero], axis=1))


def kernel(x_prompt, x_sample, state_pool, cache_mla, page_table, norm_mix, norm_ffn, w_in, g_v, w_s, b_s, w_pool, pool_scale, w_o_mix, w_ffn_gate, w_ffn_up, w_ffn_down, w_dqkv, g_q, g_kv, w_uq, w_uk, w_uv, w_o_attn, w_router, w_exp_gate, w_exp_up, w_exp_down, norm_final):
    x = jnp.concatenate([x_prompt.reshape(T_P, D_MODEL), x_sample.reshape(T_S, D_MODEL)], axis=0)

    y = _inproj(x, norm_mix[0:1], w_in[0], g_v)
    b_bcast = jnp.broadcast_to(b_s[0][:, :, None], (A_HEADS, CHUNK, CHUNK))
    a_out = _gate(y, w_s[0], b_bcast)
    hist2d = state_pool[0].reshape(T_S, POOL_HIST * B_WIDTH)
    b_out = _pool(y, hist2d, w_pool[0], pool_scale[0])
    h = _proj_res([a_out, b_out], w_o_mix[0], x, "mix_out")
    h = _ffn(h, norm_ffn[0:1], w_ffn_gate[0], w_ffn_up[0], w_ffn_down[0])

    z = y[2]
    z_p = z[:T_P].reshape(BATCH, SEQ, B_WIDTH)
    pool_state_prompt = z_p[:, SEQ - POOL_HIST:][None]
    z_s = z[T_P:]
    pool_state_sample = jnp.concatenate([state_pool[0][:, 1:], z_s[:, None, :]], axis=1)[None]
    chunk_v_sample = y[1, T_P:].reshape(1, T_S, 1, A_WIDTH)

    cos_t, sin_t = _rope_tables()
    w_dqkv_pad = jnp.pad(w_dqkv[0], ((0, 0), (0, 128 - QK_ROPE)))
    cq, ckv, pe, lat = _dqkv(h, norm_mix[1:2], w_dqkv_pad, g_q, g_kv, cos_t, sin_t)
    wq_pad = jnp.pad(w_uq[0].reshape(Q_LORA, N_HEADS, QK_NOPE + QK_ROPE),
                     ((0, 0), (0, 0), (0, HEAD_PAD - QK_NOPE - QK_ROPE))).reshape(Q_LORA, N_HEADS * HEAD_PAD)
    wk2d = w_uk[0].reshape(KV_LORA, N_HEADS * QK_NOPE)
    wv2d = w_uv[0].reshape(KV_LORA, N_HEADS * V_HEAD)
    q, k, v = _qkv(cq, ckv, pe, cos_t, sin_t, wq_pad, wk2d, wv2d)
    attn = _flash(q, k, v)

    qlat = _qlat(q, wk2d)
    olat = _decode(page_table.reshape(-1),
                   qlat.reshape(T_S, N_HEADS, KV_LORA),
                   q.reshape(T, N_HEADS, HEAD_PAD),
                   lat.reshape(T, 1, LATENT),
                   cache_mla[0])
    attn_s = _ouv(olat.reshape(T_S, N_HEADS * KV_LORA), wv2d)
    attn = jnp.concatenate([attn, attn_s], axis=0)
    h = _proj_res([attn], w_o_attn[0], h, "attn_out")

    mla_rows_prompt = lat[:T_P].reshape(1, BATCH, SEQ, LATENT)
    mla_rows_sample = lat[T_P:].reshape(1, T_S, 1, LATENT)

    w_router_pad = jnp.pad(w_router[0], ((0, 0), (0, 128 - N_EXPERTS)))
    meta = _router(h, norm_ffn[1:2], w_router_pad)
    experts = meta[:, :TOP_K].astype(jnp.int32)
    te, rows, src, pos = _routing_tables(experts)
    y_sorted = _moe(te, rows, src, h, norm_ffn[1:2],
                    w_exp_gate[0], w_exp_up[0], w_exp_down[0])
    y_p, y_s = _combine(pos, h, meta, norm_final.reshape(1, D_MODEL), y_sorted)

    return (y_p.reshape(BATCH, SEQ, D_MODEL), y_s.reshape(T_S, 1, D_MODEL),
            pool_state_prompt, pool_state_sample, chunk_v_sample,
            mla_rows_prompt, mla_rows_sample)
```

```python
import functools

import numpy as np
import jax
import jax.numpy as jnp
from jax import lax
from jax.experimental import pallas as pl
from jax.experimental.pallas import tpu as pltpu

D_MODEL = 2048
BATCH = 4
SEQ = 2048
DEC_BATCH = 128
PAST_LEN = 8192
PAGE_SIZE = 128
N_PAGES = PAST_LEN // PAGE_SIZE
A_WIDTH = 1024
A_HEADS = 8
A_HEAD_DIM = 128
CHUNK = 128
B_WIDTH = 1024
POOL_WINDOWS = (2, 4, 8, 16)
B_GROUP_DIM = 256
POOL_HIST = 15
N_HEADS = 16
QK_NOPE = 128
QK_ROPE = 64
V_HEAD = 128
Q_LORA = 512
KV_LORA = 512
LATENT = KV_LORA + QK_ROPE
ROPE_THETA = 10000.0
ATTN_SCALE = (QK_NOPE + QK_ROPE) ** -0.5
D_FF = 7168
N_EXPERTS = 8
TOP_K = 2
EPS = 1e-6

T_P = BATCH * SEQ
T_S = DEC_BATCH
T = T_P + T_S

BF = jnp.bfloat16
F32 = jnp.float32

VMEM_LIMIT_BYTES = 56 * 1024 * 1024

TM = 640
N_TM = T // TM
TM_FFN = 1040
TF = 256
HEAD_PAD = 256
TQ = 512
N_QT = SEQ // TQ
PAGES_PER_STEP = 16
N_KV_STEPS = N_PAGES // PAGES_PER_STEP
MOE_TM = 1024
MOE_SUB = 256
MOE_NT = (T * TOP_K) // MOE_TM + N_EXPERTS
MOE_ROWS = MOE_NT * MOE_TM
TC = 128
N_TC = T // TC
N_TC_P = T_P // TC


def _cparams(sem):
    return pltpu.CompilerParams(dimension_semantics=sem, vmem_limit_bytes=VMEM_LIMIT_BYTES)


def _rms(x, g):
    return x * lax.rsqrt(jnp.mean(x * x, axis=-1, keepdims=True) + EPS) * g


def _gelu(x):
    return 0.5 * x * (1.0 + lax.erf(x * np.float32(np.sqrt(0.5))))


def _bdot(a, b):
    return jnp.dot(a, b, preferred_element_type=F32)


def _bdot_nt(a, b):
    return lax.dot_general(a, b, (((1,), (1,)), ((), ())), preferred_element_type=F32)


def _rope128(x, c, s):
    swapped = pltpu.roll(x, 96, axis=1) + pltpu.roll(x, 32, axis=1)
    return x * c + swapped * s


def _inproj_body(x_ref, g_ref, w_ref, gv_ref, o_ref):
    j = pl.program_id(0)
    xn = _rms(x_ref[...], g_ref[...]).astype(BF)
    y = _bdot(xn, w_ref[...].astype(BF))

    @pl.when(j == 0)
    def _():
        o_ref[0] = _gelu(y)

    @pl.when(j == 1)
    def _():
        o_ref[0] = _rms(_gelu(y), gv_ref[...])

    @pl.when(j == 2)
    def _():
        o_ref[0] = y


def _inproj(x, g, w, gv):
    return pl.pallas_call(
        _inproj_body,
        grid=(3, N_TM),
        in_specs=[
            pl.BlockSpec((TM, D_MODEL), lambda j, i: (i, 0)),
            pl.BlockSpec((1, D_MODEL), lambda j, i: (0, 0)),
            pl.BlockSpec((D_MODEL, A_WIDTH), lambda j, i: (0, j)),
            pl.BlockSpec((1, A_WIDTH), lambda j, i: (0, 0)),
        ],
        out_specs=pl.BlockSpec((1, TM, A_WIDTH), lambda j, i: (j, i, 0)),
        out_shape=jax.ShapeDtypeStruct((3, T, A_WIDTH), F32),
        compiler_params=_cparams(("arbitrary", "arbitrary")),
        name="inproj",
    )(x, g, w, gv)


def _gate_body(u_ref, v_ref, ws_ref, bb_ref, a_ref):
    is_sample = pl.program_id(0) >= N_TC_P
    row = lax.broadcasted_iota(jnp.int32, (CHUNK, CHUNK), 0)
    col = lax.broadcasted_iota(jnp.int32, (CHUNK, CHUNK), 1)
    for h in range(A_HEADS):
        w = ws_ref[h]
        w_chunk = jnp.where(col <= row, w, 0.0)
        w_single = jnp.where(col == row, jnp.broadcast_to(w[0:1, 0:1], (CHUNK, CHUNK)), 0.0)
        w_eff = jnp.where(is_sample, w_single, w_chunk).astype(BF)
        b = bb_ref[h]
        b_eff = jnp.where(is_sample, jnp.broadcast_to(b[0:1, :], (CHUNK, CHUNK)), b)
        sl = slice(h * A_HEAD_DIM, (h + 1) * A_HEAD_DIM)
        mixed = _bdot(w_eff, v_ref[0, :, sl].astype(BF)) + b_eff
        a_ref[:, sl] = (u_ref[0, :, sl] * mixed).astype(BF)


def _gate(y, w_s, b_bcast):
    return pl.pallas_call(
        _gate_body,
        grid=(N_TC,),
        in_specs=[
            pl.BlockSpec((1, TC, A_WIDTH), lambda i: (0, i, 0)),
            pl.BlockSpec((1, TC, A_WIDTH), lambda i: (1, i, 0)),
            pl.BlockSpec((A_HEADS, CHUNK, CHUNK), lambda i: (0, 0, 0)),
            pl.BlockSpec((A_HEADS, CHUNK, CHUNK), lambda i: (0, 0, 0)),
        ],
        out_specs=pl.BlockSpec((TC, A_WIDTH), lambda i: (i, 0)),
        out_shape=jax.ShapeDtypeStruct((T, A_WIDTH), BF),
        compiler_params=_cparams(("arbitrary",)),
        name="gate",
    )(y, y, w_s, b_bcast)


def _pool_body(z_ref, hist_ref, wp_ref, ps_ref, o_ref):
    b = pl.program_id(0)

    def project(pooled, g):
        out = _bdot(pooled.astype(BF), wp_ref[g].astype(BF)) * ps_ref[g:g + 1, :]
        return out.astype(BF)

    @pl.when(b < BATCH)
    def _():
        row = lax.broadcasted_iota(jnp.int32, (SEQ, B_GROUP_DIM), 0)
        for g, w in enumerate(POOL_WINDOWS):
            sl = slice(g * B_GROUP_DIM, (g + 1) * B_GROUP_DIM)
            z = z_ref[0, :, sl]
            s = z
            k = 1
            while k < w:
                s = s + jnp.where(row >= k, pltpu.roll(s, k, axis=0), 0.0)
                k *= 2
            cnt = jnp.minimum(row + 1, w).astype(F32)
            o_ref[:, sl] = project(s / cnt - z, g)

    @pl.when(b == BATCH)
    def _():
        for g, w in enumerate(POOL_WINDOWS):
            sl = slice(g * B_GROUP_DIM, (g + 1) * B_GROUP_DIM)
            z = z_ref[0, 0:T_S, sl]
            s = z
            for k in range(POOL_HIST + 1 - w, POOL_HIST):
                s = s + hist_ref[:, k * B_WIDTH + g * B_GROUP_DIM:k * B_WIDTH + (g + 1) * B_GROUP_DIM]
            o_ref[0:T_S, sl] = project(s / np.float32(w) - z, g)


def _pool(y, hist2d, w_pool, pool_scale):
    return pl.pallas_call(
        _pool_body,
        grid=(BATCH + 1,),
        in_specs=[
            pl.BlockSpec((1, SEQ, B_WIDTH), lambda b: (2, b, 0)),
            pl.BlockSpec((T_S, POOL_HIST * B_WIDTH), lambda b: (0, 0)),
            pl.BlockSpec((len(POOL_WINDOWS), B_GROUP_DIM, B_GROUP_DIM), lambda b: (0, 0, 0)),
            pl.BlockSpec((len(POOL_WINDOWS), B_GROUP_DIM), lambda b: (0, 0)),
        ],
        out_specs=pl.BlockSpec((SEQ, B_WIDTH), lambda b: (b, 0)),
        out_shape=jax.ShapeDtypeStruct((T, B_WIDTH), BF),
        compiler_params=_cparams(("arbitrary",)),
        name="pool",
    )(y, hist2d, w_pool, pool_scale)


def _proj_res_body(*refs, n_in):
    a_refs = refs[:n_in]
    w_ref, r_ref, o_ref = refs[n_in:]
    acc = r_ref[...]
    off = 0
    for a_ref in a_refs:
        k = a_ref.shape[1]
        acc = acc + _bdot(a_ref[...], w_ref[off:off + k, :].astype(BF))
        off += k
    o_ref[...] = acc


def _proj_res(acts, w, res, name):
    tn = 1024
    k_total = w.shape[0]
    return pl.pallas_call(
        functools.partial(_proj_res_body, n_in=len(acts)),
        grid=(D_MODEL // tn, N_TM),
        in_specs=[pl.BlockSpec((TM, a.shape[1]), lambda j, i: (i, 0)) for a in acts] + [
            pl.BlockSpec((k_total, tn), lambda j, i: (0, j)),
            pl.BlockSpec((TM, tn), lambda j, i: (i, j)),
        ],
        out_specs=pl.BlockSpec((TM, tn), lambda j, i: (i, j)),
        out_shape=jax.ShapeDtypeStruct((T, D_MODEL), F32),
        compiler_params=_cparams(("arbitrary", "arbitrary")),
        name=name,
    )(*acts, w, res)


def _ffn_body(x_hbm, g_ref, wg_ref, wu_ref, wd_ref, o_ref, xn_ref, sem):
    i = pl.program_id(0)
    f = pl.program_id(1)

    @pl.when(f == 0)
    def _():
        cp = pltpu.make_async_copy(x_hbm.at[pl.ds(i * TM_FFN, TM_FFN)], o_ref, sem)
        cp.start()
        cp.wait()
        xn_ref[...] = _rms(o_ref[...], g_ref[...]).astype(BF)

    xn = xn_ref[...]
    gate = _bdot(xn, wg_ref[...].astype(BF))
    up = _bdot(xn, wu_ref[...].astype(BF))
    act = (gate * jax.nn.sigmoid(gate) * up).astype(BF)
    o_ref[...] += _bdot(act, wd_ref[...].astype(BF))


def _ffn(x, g, wg, wu, wd):
    return pl.pallas_call(
        _ffn_body,
        grid=(T // TM_FFN, D_FF // TF),
        in_specs=[
            pl.BlockSpec(memory_space=pl.ANY),
            pl.BlockSpec((1, D_MODEL), lambda i, f: (0, 0)),
            pl.BlockSpec((D_MODEL, TF), lambda i, f: (0, f)),
            pl.BlockSpec((D_MODEL, TF), lambda i, f: (0, f)),
            pl.BlockSpec((TF, D_MODEL), lambda i, f: (f, 0)),
        ],
        out_specs=pl.BlockSpec((TM_FFN, D_MODEL), lambda i, f: (i, 0)),
        out_shape=jax.ShapeDtypeStruct((T, D_MODEL), F32),
        scratch_shapes=[pltpu.VMEM((TM_FFN, D_MODEL), BF), pltpu.SemaphoreType.DMA(())],
        compiler_params=_cparams(("arbitrary", "arbitrary")),
        name="ffn",
    )(x, g, wg, wu, wd)


def _dqkv_body(x_ref, g_ref, w_ref, gq_ref, gkv_ref, cos_ref, sin_ref,
               cq_ref, ckv_ref, pe_ref, lat_ref):
    xn = _rms(x_ref[...], g_ref[...]).astype(BF)
    d = _bdot(xn, w_ref[...].astype(BF))
    cq_ref[...] = _rms(d[:, :Q_LORA], gq_ref[...]).astype(BF)
    ckv = _rms(d[:, Q_LORA:Q_LORA + KV_LORA], gkv_ref[...])
    pe = _rope128(d[:, Q_LORA + KV_LORA:], cos_ref[...], sin_ref[...])
    ckv_ref[...] = ckv.astype(BF)
    pe_ref[...] = pe.astype(BF)
    lat_ref[:, :KV_LORA] = ckv
    lat_ref[:, KV_LORA:] = pe[:, :QK_ROPE]


def _dqkv(x, g, w_pad, gq, gkv, cos_t, sin_t):
    n = w_pad.shape[1]
    row = lambda i: (i, 0)
    fix = lambda i: (0, 0)
    return pl.pallas_call(
        _dqkv_body,
        grid=(N_TM,),
        in_specs=[
            pl.BlockSpec((TM, D_MODEL), row),
            pl.BlockSpec((1, D_MODEL), fix),
            pl.BlockSpec((D_MODEL, n), fix),
            pl.BlockSpec((1, Q_LORA), fix),
            pl.BlockSpec((1, KV_LORA), fix),
            pl.BlockSpec((TM, 128), row),
            pl.BlockSpec((TM, 128), row),
        ],
        out_specs=[
            pl.BlockSpec((TM, Q_LORA), row),
            pl.BlockSpec((TM, KV_LORA), row),
            pl.BlockSpec((TM, 128), row),
            pl.BlockSpec((TM, LATENT), row),
        ],
        out_shape=[
            jax.ShapeDtypeStruct((T, Q_LORA), BF),
            jax.ShapeDtypeStruct((T, KV_LORA), BF),
            jax.ShapeDtypeStruct((T, 128), BF),
            jax.ShapeDtypeStruct((T, LATENT), F32),
        ],
        compiler_params=_cparams(("arbitrary",)),
        name="dqkv",
    )(x, g, w_pad, gq, gkv, cos_t, sin_t)


HG = 4


def _qkv_body(cq_ref, ckv_ref, pe_ref, cos_ref, sin_ref, wq_ref, wk_ref, wvt_ref,
              q_ref, k_ref, vt_ref):
    q = _bdot(cq_ref[...], wq_ref[...].astype(BF)) * np.float32(ATTN_SCALE)
    ckv = ckv_ref[...]
    kn = _bdot(ckv, wk_ref[...].astype(BF))
    pe = pe_ref[...]
    c = cos_ref[...]
    s = sin_ref[...]
    for h in range(HG):
        o = h * HEAD_PAD
        q_ref[:, o:o + QK_NOPE] = q[:, o:o + QK_NOPE].astype(BF)
        q_ref[:, o + QK_NOPE:o + HEAD_PAD] = _rope128(q[:, o + QK_NOPE:o + HEAD_PAD], c, s).astype(BF)
        k_ref[:, o:o + QK_NOPE] = kn[:, h * QK_NOPE:(h + 1) * QK_NOPE].astype(BF)
        k_ref[:, o + QK_NOPE:o + HEAD_PAD] = pe
    vt_ref[...] = _bdot_nt(wvt_ref[...].astype(BF), ckv).astype(BF)


def _qkv(cq, ckv, pe, cos_t, sin_t, wq_pad, wk, wv_t):
    row = lambda j, i: (i, 0)
    col = lambda j, i: (0, j)
    out = lambda j, i: (i, j)
    return pl.pallas_call(
        _qkv_body,
        grid=(N_HEADS // HG, N_TM),
        in_specs=[
            pl.BlockSpec((TM, Q_LORA), row),
            pl.BlockSpec((TM, KV_LORA), row),
            pl.BlockSpec((TM, 128), row),
            pl.BlockSpec((TM, 128), row),
            pl.BlockSpec((TM, 128), row),
            pl.BlockSpec((Q_LORA, HG * HEAD_PAD), col),
            pl.BlockSpec((KV_LORA, HG * QK_NOPE), col),
            pl.BlockSpec((HG * V_HEAD, KV_LORA), lambda j, i: (j, 0)),
        ],
        out_specs=[
            pl.BlockSpec((TM, HG * HEAD_PAD), out),
            pl.BlockSpec((TM, HG * HEAD_PAD), out),
            pl.BlockSpec((HG * V_HEAD, TM), lambda j, i: (j, i)),
        ],
        out_shape=[
            jax.ShapeDtypeStruct((T, N_HEADS * HEAD_PAD), BF),
            jax.ShapeDtypeStruct((T, N_HEADS * HEAD_PAD), BF),
            jax.ShapeDtypeStruct((N_HEADS * V_HEAD, T), BF),
        ],
        compiler_params=_cparams(("arbitrary", "arbitrary")),
        name="qkv",
    )(cq, ckv, pe, cos_t, sin_t, wq_pad, wk, wv_t)


_PAIRS = [(qi, ki) for qi in range(N_QT) for ki in range(qi + 1)]


def _flash_body(qi_tab, ki_tab, q_ref, k_ref, vt_ref, o_ref, m_ref, l_ref, acc_ref):
    p = pl.program_id(1)
    qi = qi_tab[p]
    ki = ki_tab[p]

    @pl.when(ki == 0)
    def _():
        m_ref[...] = jnp.full(m_ref.shape, -jnp.inf, F32)
        l_ref[...] = jnp.zeros(l_ref.shape, F32)
        acc_ref[...] = jnp.zeros(acc_ref.shape, F32)

    key = lax.broadcasted_iota(jnp.int32, (TQ, TQ), 0)
    qry = lax.broadcasted_iota(jnp.int32, (TQ, TQ), 1)
    visible = key - qry <= (qi - ki) * TQ
    for h in range(N_HEADS):
        qk = slice(h * HEAD_PAD, (h + 1) * HEAD_PAD)
        vs = slice(h * V_HEAD, (h + 1) * V_HEAD)
        st = _bdot_nt(k_ref[:, qk], q_ref[:, qk])
        st = jnp.where(visible, st, -jnp.inf)
        m_prev = m_ref[h]
        m_new = jnp.maximum(m_prev, jnp.max(st, axis=0, keepdims=True))
        alpha = jnp.exp(m_prev - m_new)
        pt = jnp.exp(st - m_new)
        l_ref[h] = alpha * l_ref[h] + jnp.sum(pt, axis=0, keepdims=True)
        acc_ref[vs, :] = alpha * acc_ref[vs, :] + _bdot(vt_ref[vs, :], pt.astype(BF))
        m_ref[h] = m_new

    @pl.when(ki == qi)
    def _():
        for h in range(N_HEADS):
            vs = slice(h * V_HEAD, (h + 1) * V_HEAD)
            o_ref[:, vs] = (acc_ref[vs, :] / l_ref[h]).T.astype(BF)


def _flash(q, k, vt):
    qi_tab = jnp.asarray(np.array([p[0] for p in _PAIRS], np.int32))
    ki_tab = jnp.asarray(np.array([p[1] for p in _PAIRS], np.int32))
    grid_spec = pltpu.PrefetchScalarGridSpec(
        num_scalar_prefetch=2,
        grid=(BATCH, len(_PAIRS)),
        in_specs=[
            pl.BlockSpec((TQ, N_HEADS * HEAD_PAD), lambda b, p, qt, kt: (b * N_QT + qt[p], 0)),
            pl.BlockSpec((TQ, N_HEADS * HEAD_PAD), lambda b, p, qt, kt: (b * N_QT + kt[p], 0)),
            pl.BlockSpec((N_HEADS * V_HEAD, TQ), lambda b, p, qt, kt: (0, b * N_QT + kt[p])),
        ],
        out_specs=pl.BlockSpec((TQ, N_HEADS * V_HEAD), lambda b, p, qt, kt: (b * N_QT + qt[p], 0)),
        scratch_shapes=[
            pltpu.VMEM((N_HEADS, 1, TQ), F32),
            pltpu.VMEM((N_HEADS, 1, TQ), F32),
            pltpu.VMEM((N_HEADS * V_HEAD, TQ), F32),
        ],
    )
    return pl.pallas_call(
        _flash_body,
        grid_spec=grid_spec,
        out_shape=jax.ShapeDtypeStruct((T_P, N_HEADS * V_HEAD), BF),
        compiler_params=_cparams(("arbitrary", "arbitrary")),
        name="flash",
    )(qi_tab, ki_tab, q, k, vt)


def _qlat_body(q_ref, wk_ref, o_ref):
    o_ref[...] = _bdot_nt(q_ref[:, :QK_NOPE], wk_ref[...].astype(BF)).astype(BF)


def _qlat(q, wk):
    return pl.pallas_call(
        _qlat_body,
        grid=(N_HEADS,),
        in_specs=[
            pl.BlockSpec((T_S, HEAD_PAD), lambda h: (T_P // T_S, h)),
            pl.BlockSpec((KV_LORA, QK_NOPE), lambda h: (0, h)),
        ],
        out_specs=pl.BlockSpec((T_S, KV_LORA), lambda h: (0, h)),
        out_shape=jax.ShapeDtypeStruct((T_S, N_HEADS * KV_LORA), BF),
        compiler_params=_cparams(("arbitrary",)),
        name="qlat",
    )(q, wk)


def _decode_body(pt_ref, qlat_ref, q_ref, lat_ref, cache_hbm, o_ref,
                 buf, kb_ref, sems, m_ref, l_ref, acc_ref):
    b = pl.program_id(0)
    c = pl.program_id(1)
    step = b * N_KV_STEPS + c
    n_steps = DEC_BATCH * N_KV_STEPS
    slot = step % 2

    def page_copy(st, j, sl):
        pg = pt_ref[st * PAGES_PER_STEP + j]
        return pltpu.make_async_copy(cache_hbm.at[pg], buf.at[sl, j], sems.at[sl])

    def start_fetch(st, sl):
        for j in range(PAGES_PER_STEP):
            page_copy(st, j, sl).start()

    @pl.when(step == 0)
    def _():
        start_fetch(0, 0)

    @pl.when(step + 1 < n_steps)
    def _():
        start_fetch(step + 1, 1 - slot)

    qlat = qlat_ref[0]
    qpe = q_ref[0, :, QK_NOPE:QK_NOPE + QK_ROPE]

    @pl.when(c == 0)
    def _():
        lat = lat_ref[pl.ds(b, 1), :]
        s_self = (jnp.sum(qlat.astype(F32) * lat[:, :KV_LORA], axis=-1, keepdims=True)
                  + jnp.sum(qpe.astype(F32) * lat[:, KV_LORA:], axis=-1, keepdims=True))
        m_ref[...] = s_self
        l_ref[...] = jnp.ones(l_ref.shape, F32)
        acc_ref[...] = jnp.broadcast_to(lat[:, :KV_LORA], acc_ref.shape)

    for j in range(PAGES_PER_STEP):
        page_copy(step, j, slot).wait()

    s_parts = []
    for j in range(PAGES_PER_STEP):
        kb_ref[j] = buf[slot, j].astype(BF)
        s_parts.append(_bdot(qlat, kb_ref[j, :KV_LORA, :]) + _bdot(qpe, kb_ref[j, KV_LORA:, :]))
    s = jnp.concatenate(s_parts, axis=1)
    m_prev = m_ref[...]
    m_new = jnp.maximum(m_prev, jnp.max(s, axis=-1, keepdims=True))
    alpha = jnp.exp(m_prev - m_new)
    pexp = jnp.exp(s - m_new)
    l_ref[...] = alpha * l_ref[...] + jnp.sum(pexp, axis=-1, keepdims=True)
    pb = pexp.astype(BF)
    pv = jnp.zeros(acc_ref.shape, F32)
    for j in range(PAGES_PER_STEP):
        pv = pv + _bdot_nt(pb[:, j * PAGE_SIZE:(j + 1) * PAGE_SIZE], kb_ref[j, :KV_LORA, :])
    acc_ref[...] = alpha * acc_ref[...] + pv
    m_ref[...] = m_new

    @pl.when(c == N_KV_STEPS - 1)
    def _():
        o_ref[0] = acc_ref[...] / l_ref[...]


def _decode(page_table_flat, qlat3, q3, lat_s, cache_t):
    grid_spec = pltpu.PrefetchScalarGridSpec(
        num_scalar_prefetch=1,
        grid=(DEC_BATCH, N_KV_STEPS),
        in_specs=[
            pl.BlockSpec((1, N_HEADS, KV_LORA), lambda b, c, pt: (b, 0, 0)),
            pl.BlockSpec((1, N_HEADS, HEAD_PAD), lambda b, c, pt: (b, 0, 0)),
            pl.BlockSpec((T_S, LATENT), lambda b, c, pt: (0, 0)),
            pl.BlockSpec(memory_space=pl.ANY),
        ],
        out_specs=pl.BlockSpec((1, N_HEADS, KV_LORA), lambda b, c, pt: (b, 0, 0)),
        scratch_shapes=[
            pltpu.VMEM((2, PAGES_PER_STEP, LATENT, PAGE_SIZE), F32),
            pltpu.VMEM((PAGES_PER_STEP, LATENT, PAGE_SIZE), BF),
            pltpu.SemaphoreType.DMA((2,)),
            pltpu.VMEM((N_HEADS, 1), F32),
            pltpu.VMEM((N_HEADS, 1), F32),
            pltpu.VMEM((N_HEADS, KV_LORA), F32),
        ],
    )
    return pl.pallas_call(
        _decode_body,
        grid_spec=grid_spec,
        out_shape=jax.ShapeDtypeStruct((DEC_BATCH, N_HEADS, KV_LORA), F32),
        compiler_params=_cparams(("arbitrary", "arbitrary")),
        name="decode",
    )(page_table_flat, qlat3, q3, lat_s, cache_t)


def _ouv_body(o_ref, wv_ref, out_ref):
    out_ref[...] = _bdot(o_ref[...].astype(BF), wv_ref[...].astype(BF)).astype(BF)


def _ouv(olat2d, wv):
    return pl.pallas_call(
        _ouv_body,
        grid=(N_HEADS,),
        in_specs=[
            pl.BlockSpec((T_S, KV_LORA), lambda h: (0, h)),
            pl.BlockSpec((KV_LORA, V_HEAD), lambda h: (0, h)),
        ],
        out_specs=pl.BlockSpec((T_S, V_HEAD), lambda h: (0, h)),
        out_shape=jax.ShapeDtypeStruct((T_S, N_HEADS * V_HEAD), BF),
        compiler_params=_cparams(("arbitrary",)),
        name="ouv",
    )(olat2d, wv)


def _router_body(x_ref, g_ref, w_ref, meta_ref):
    xn = _rms(x_ref[...], g_ref[...])
    w = w_ref[...]
    xh = xn.astype(BF)
    xl = (xn - xh.astype(F32)).astype(BF)
    wh = w.astype(BF)
    wl = (w - wh.astype(F32)).astype(BF)
    logits = _bdot(xh, wh) + (_bdot(xh, wl) + _bdot(xl, wh))
    lane = lax.broadcasted_iota(jnp.int32, logits.shape, 1).astype(F32)
    lg = jnp.where(lane < N_EXPERTS, logits, -jnp.inf)
    t1 = jnp.max(lg, axis=-1, keepdims=True)
    i1 = jnp.min(jnp.where(lg == t1, lane, 128.0), axis=-1, keepdims=True)
    lg2 = jnp.where(lane == i1, -jnp.inf, lg)
    t2 = jnp.max(lg2, axis=-1, keepdims=True)
    i2 = jnp.min(jnp.where(lg2 == t2, lane, 128.0), axis=-1, keepdims=True)
    e = jnp.exp(t2 - t1)
    g1 = 1.0 / (1.0 + e)
    g2 = e / (1.0 + e)
    meta = jnp.where(lane == 0, i1, jnp.where(lane == 1, i2, jnp.where(lane == 2, g1, g2)))
    meta_ref[...] = meta


def _router(x, g, w_pad):
    return pl.pallas_call(
        _router_body,
        grid=(N_TM,),
        in_specs=[
            pl.BlockSpec((TM, D_MODEL), lambda i: (i, 0)),
            pl.BlockSpec((1, D_MODEL), lambda i: (0, 0)),
            pl.BlockSpec((D_MODEL, 128), lambda i: (0, 0)),
        ],
        out_specs=pl.BlockSpec((TM, 128), lambda i: (i, 0)),
        out_shape=jax.ShapeDtypeStruct((T, 128), F32),
        compiler_params=_cparams(("arbitrary",)),
        name="router",
    )(x, g, w_pad)


N_SUB = MOE_TM // MOE_SUB
N_FF = D_FF // TF


def _moe_body(te_ref, rows_ref, src_ref, h_hbm, g_ref, wg_ref, wu_ref, wd_ref,
              y_ref, xs_ref, xn_ref, sems):
    t = pl.program_id(0)
    f = pl.program_id(1)
    nrows = rows_ref[t]

    @pl.when(f == 0)
    def _():
        for sb in range(N_SUB):
            @pl.when(sb * MOE_SUB < nrows)
            def _():
                def issue(r, carry):
                    tok = src_ref[t * MOE_TM + sb * MOE_SUB + r]
                    pltpu.make_async_copy(h_hbm.at[pl.ds(tok, 1)],
                                          xs_ref.at[pl.ds(sb * MOE_SUB + r, 1)], sems.at[sb]).start()
                    return carry
                lax.fori_loop(0, MOE_SUB, issue, 0)
        for sb in range(N_SUB):
            rs = pl.ds(sb * MOE_SUB, MOE_SUB)

            @pl.when(sb * MOE_SUB < nrows)
            def _():
                pltpu.make_async_copy(h_hbm.at[pl.ds(0, MOE_SUB)], xs_ref.at[rs], sems.at[sb]).wait()
                xn_ref[rs, :] = _rms(xs_ref[rs, :], g_ref[...]).astype(BF)

            @pl.when(sb * MOE_SUB >= nrows)
            def _():
                y_ref[rs, :] = jnp.zeros((MOE_SUB, D_MODEL), F32)

    @pl.when(nrows > 0)
    def _():
        wg = wg_ref[0].astype(BF)
        wu = wu_ref[0].astype(BF)
        wd = wd_ref[0].astype(BF)
        for sb in range(N_SUB):
            rs = pl.ds(sb * MOE_SUB, MOE_SUB)

            @pl.when(sb * MOE_SUB < nrows)
            def _():
                xn = xn_ref[rs, :]
                gate = _bdot(xn, wg)
                up = _bdot(xn, wu)
                act = (gate * jax.nn.sigmoid(gate) * up).astype(BF)
                out = _bdot(act, wd)

                @pl.when(f == 0)
                def _():
                    y_ref[rs, :] = out

                @pl.when(f > 0)
                def _():
                    y_ref[rs, :] += out


def _moe(tile_expert, tile_rows, src, h, g, wg, wu, wd):
    ff = lambda t, f, nr: jnp.where(nr[t] > 0, f, N_FF - 1)
    grid_spec = pltpu.PrefetchScalarGridSpec(
        num_scalar_prefetch=3,
        grid=(MOE_NT, N_FF),
        in_specs=[
            pl.BlockSpec(memory_space=pl.ANY),
            pl.BlockSpec((1, D_MODEL), lambda t, f, te, nr, sr: (0, 0)),
            pl.BlockSpec((1, D_MODEL, TF), lambda t, f, te, nr, sr: (te[t], 0, ff(t, f, nr))),
            pl.BlockSpec((1, D_MODEL, TF), lambda t, f, te, nr, sr: (te[t], 0, ff(t, f, nr))),
            pl.BlockSpec((1, TF, D_MODEL), lambda t, f, te, nr, sr: (te[t], ff(t, f, nr), 0)),
        ],
        out_specs=pl.BlockSpec((MOE_TM, D_MODEL), lambda t, f, te, nr, sr: (t, 0)),
        scratch_shapes=[
            pltpu.VMEM((MOE_TM, D_MODEL), F32),
            pltpu.VMEM((MOE_TM, D_MODEL), BF),
            pltpu.SemaphoreType.DMA((N_SUB,)),
        ],
    )
    return pl.pallas_call(
        _moe_body,
        grid_spec=grid_spec,
        out_shape=jax.ShapeDtypeStruct((MOE_ROWS, D_MODEL), F32),
        compiler_params=_cparams(("arbitrary", "arbitrary")),
        name="moe",
    )(tile_expert, tile_rows, src, h, g, wg, wu, wd)


def _combine_body(pos_ref, h_ref, meta_ref, gf_ref, y_hbm, yp_ref, ys_ref, yb_ref, sem):
    i = pl.program_id(0)

    def issue(r, carry):
        tok = i * TC + r
        for k in range(TOP_K):
            pltpu.make_async_copy(y_hbm.at[pl.ds(pos_ref[TOP_K * tok + k], 1)],
                                  yb_ref.at[k, pl.ds(r, 1)], sem).start()
        return carry

    lax.fori_loop(0, TC, issue, 0)
    for k in range(TOP_K):
        pltpu.make_async_copy(y_hbm.at[pl.ds(0, TC)], yb_ref.at[k], sem).wait()
    meta = meta_ref[...]
    out = h_ref[...] + meta[:, 2:3] * yb_ref[0] + meta[:, 3:4] * yb_ref[1]
    out = _rms(out, gf_ref[...])

    @pl.when(i < N_TC_P)
    def _():
        yp_ref[...] = out

    @pl.when(i >= N_TC_P)
    def _():
        ys_ref[...] = out


def _combine(pos, h, meta, gf, y):
    grid_spec = pltpu.PrefetchScalarGridSpec(
        num_scalar_prefetch=1,
        grid=(N_TC,),
        in_specs=[
            pl.BlockSpec((TC, D_MODEL), lambda i, pos: (i, 0)),
            pl.BlockSpec((TC, 128), lambda i, pos: (i, 0)),
            pl.BlockSpec((1, D_MODEL), lambda i, pos: (0, 0)),
            pl.BlockSpec(memory_space=pl.ANY),
        ],
        out_specs=[
            pl.BlockSpec((TC, D_MODEL), lambda i, pos: (jnp.minimum(i, N_TC_P - 1), 0)),
            pl.BlockSpec((T_S, D_MODEL), lambda i, pos: (0, 0)),
        ],
        scratch_shapes=[
            pltpu.VMEM((TOP_K, TC, D_MODEL), F32),
            pltpu.SemaphoreType.DMA(()),
        ],
    )
    return pl.pallas_call(
        _combine_body,
        grid_spec=grid_spec,
        out_shape=[
            jax.ShapeDtypeStruct((T_P, D_MODEL), F32),
            jax.ShapeDtypeStruct((T_S, D_MODEL), F32),
        ],
        compiler_params=_cparams(("arbitrary",)),
        name="combine",
    )(pos, h, meta, gf, y)


def _routing_tables(experts):
    flat = experts.reshape(-1)
    onehot = (flat[:, None] == jnp.arange(N_EXPERTS, dtype=jnp.int32)[None, :]).astype(jnp.int32)
    csum = jnp.cumsum(onehot, axis=0)
    rank = jnp.take_along_axis(csum, flat[:, None], axis=1)[:, 0] - 1
    counts = csum[-1]
    tiles_per = (counts + MOE_TM - 1) // MOE_TM
    tile_end = jnp.cumsum(tiles_per)
    tile_start = tile_end - tiles_per
    n_used = tile_end[-1]
    pos = tile_start[flat] * MOE_TM + rank
    tid = jnp.arange(MOE_NT, dtype=jnp.int32)
    last_tile = jnp.maximum(n_used - 1, 0)
    te = jnp.sum(tile_end[None, :] <= jnp.minimum(tid, last_tile)[:, None], axis=1).astype(jnp.int32)
    te = jnp.minimum(te, N_EXPERTS - 1)
    rows = jnp.clip(counts[te] - (tid - tile_start[te]) * MOE_TM, 0, MOE_TM)
    rows = jnp.where(tid < n_used, rows, 0).astype(jnp.int32)
    tok = jnp.arange(T * TOP_K, dtype=jnp.int32) // TOP_K
    src = jnp.zeros((MOE_ROWS,), jnp.int32).at[pos].set(tok)
    return te, rows, src, pos.astype(jnp.int32)


def _rope_tables():
    half = QK_ROPE // 2
    inv = 1.0 / (ROPE_THETA ** (jnp.arange(half, dtype=F32) * 2.0 / QK_ROPE))
    pos = jnp.concatenate([jnp.tile(jnp.arange(SEQ), BATCH), jnp.full((T_S,), PAST_LEN)])
    ang = pos.astype(F32)[:, None] * inv[None, :]
    cos, sin = jnp.cos(ang), jnp.sin(ang)
    zero = jnp.zeros((T, 128 - QK_ROPE), F32)
    return (jnp.concatenate([cos, cos, zero], axis=1),
            jnp.concatenate([-sin, sin, zero], axis=1))


def kernel(x_prompt, x_sample, state_pool, cache_mla, page_table, norm_mix, norm_ffn, w_in, g_v, w_s, b_s, w_pool, pool_scale, w_o_mix, w_ffn_gate, w_ffn_up, w_ffn_down, w_dqkv, g_q, g_kv, w_uq, w_uk, w_uv, w_o_attn, w_router, w_exp_gate, w_exp_up, w_exp_down, norm_final):
    x = jnp.concatenate([x_prompt.reshape(T_P, D_MODEL), x_sample.reshape(T_S, D_MODEL)], axis=0)

    y = _inproj(x, norm_mix[0:1], w_in[0], g_v)
    b_bcast = jnp.broadcast_to(b_s[0][:, :, None], (A_HEADS, CHUNK, CHUNK))
    a_out = _gate(y, w_s[0], b_bcast)
    hist2d = state_pool[0].reshape(T_S, POOL_HIST * B_WIDTH)
    b_out = _pool(y, hist2d, w_pool[0], pool_scale[0])
    h = _proj_res([a_out, b_out], w_o_mix[0], x, "mix_out")
    h = _ffn(h, norm_ffn[0:1], w_ffn_gate[0], w_ffn_up[0], w_ffn_down[0])

    z = y[2]
    z_p = z[:T_P].reshape(BATCH, SEQ, B_WIDTH)
    pool_state_prompt = z_p[:, SEQ - POOL_HIST:][None]
    z_s = z[T_P:]
    pool_state_sample = jnp.concatenate([state_pool[0][:, 1:], z_s[:, None, :]], axis=1)[None]
    chunk_v_sample = y[1, T_P:].reshape(1, T_S, 1, A_WIDTH)

    cos_t, sin_t = _rope_tables()
    w_dqkv_pad = jnp.pad(w_dqkv[0], ((0, 0), (0, 128 - QK_ROPE)))
    cq, ckv, pe, lat = _dqkv(h, norm_mix[1:2], w_dqkv_pad, g_q, g_kv, cos_t, sin_t)
    wq_pad = jnp.pad(w_uq[0].reshape(Q_LORA, N_HEADS, QK_NOPE + QK_ROPE),
                     ((0, 0), (0, 0), (0, HEAD_PAD - QK_NOPE - QK_ROPE))).reshape(Q_LORA, N_HEADS * HEAD_PAD)
    wk2d = w_uk[0].reshape(KV_LORA, N_HEADS * QK_NOPE)
    wv2d = w_uv[0].reshape(KV_LORA, N_HEADS * V_HEAD)
    q, k, vt = _qkv(cq, ckv, pe, cos_t, sin_t, wq_pad, wk2d, wv2d.T)
    attn = _flash(q, k, vt)

    qlat = _qlat(q, wk2d)
    cache_t = jnp.swapaxes(cache_mla, 2, 3).reshape(-1, LATENT, PAGE_SIZE)
    olat = _decode(page_table.reshape(-1),
                   qlat.reshape(T_S, N_HEADS, KV_LORA),
                   q[T_P:].reshape(T_S, N_HEADS, HEAD_PAD),
                   lat[T_P:],
                   cache_t)
    attn_s = _ouv(olat.reshape(T_S, N_HEADS * KV_LORA), wv2d)
    attn = jnp.concatenate([attn, attn_s], axis=0)
    h = _proj_res([attn], w_o_attn[0], h, "attn_out")

    mla_rows_prompt = lat[:T_P].reshape(1, BATCH, SEQ, LATENT)
    mla_rows_sample = lat[T_P:].reshape(1, T_S, 1, LATENT)

    w_router_pad = jnp.pad(w_router[0], ((0, 0), (0, 128 - N_EXPERTS)))
    meta = _router(h, norm_ffn[1:2], w_router_pad)
    experts = meta[:, :TOP_K].astype(jnp.int32)
    te, rows, src, pos = _routing_tables(experts)
    y_sorted = _moe(te, rows, src, h, norm_ffn[1:2],
                    w_exp_gate[0], w_exp_up[0], w_exp_down[0])
    y_p, y_s = _combine(pos, h, meta, norm_final.reshape(1, D_MODEL), y_sorted)

    return (y_p.reshape(BATCH, SEQ, D_MODEL), y_s.reshape(T_S, 1, D_MODEL),
            pool_state_prompt, pool_state_sample, chunk_v_sample,
            mla_rows_prompt, mla_rows_sample)
```

```python
import functools

import numpy as np
import jax
import jax.numpy as jnp
from jax import lax
from jax.experimental import pallas as pl
from jax.experimental.pallas import tpu as pltpu

D_MODEL = 2048
BATCH = 4
SEQ = 2048
DEC_BATCH = 128
PAST_LEN = 8192
PAGE_SIZE = 128
N_PAGES = PAST_LEN // PAGE_SIZE
A_WIDTH = 1024
A_HEADS = 8
A_HEAD_DIM = 128
CHUNK = 128
B_WIDTH = 1024
POOL_WINDOWS = (2, 4, 8, 16)
B_GROUP_DIM = 256
POOL_HIST = 15
N_HEADS = 16
QK_NOPE = 128
QK_ROPE = 64
V_HEAD = 128
Q_LORA = 512
KV_LORA = 512
LATENT = KV_LORA + QK_ROPE
ROPE_THETA = 10000.0
ATTN_SCALE = (QK_NOPE + QK_ROPE) ** -0.5
D_FF = 7168
N_EXPERTS = 8
TOP_K = 2
EPS = 1e-6

T_P = BATCH * SEQ
T_S = DEC_BATCH
T = T_P + T_S

BF = jnp.bfloat16
F32 = jnp.float32

VMEM_LIMIT_BYTES = 56 * 1024 * 1024

TM = 640
N_TM = T // TM
TM_FFN = 1040
TF = 256
HEAD_PAD = 256
TQ = 512
N_QT = SEQ // TQ
PAGES_PER_STEP = 32
N_KV_STEPS = N_PAGES // PAGES_PER_STEP
MOE_CH = 256
MOE_BLK = 2 * MOE_CH
MOE_TM = 9 * MOE_CH
MOE_NT = (T * TOP_K) // MOE_TM + N_EXPERTS
MOE_ROWS = -(-(T * TOP_K + MOE_NT * (MOE_CH - 1)) // MOE_CH) * MOE_CH
TC = 128
N_TC = T // TC
N_TC_P = T_P // TC


def _cparams(sem):
    return pltpu.CompilerParams(dimension_semantics=sem, vmem_limit_bytes=VMEM_LIMIT_BYTES)


def _rms(x, g):
    return x * lax.rsqrt(jnp.mean(x * x, axis=-1, keepdims=True) + EPS) * g


def _gelu(x):
    return 0.5 * x * (1.0 + lax.erf(x * np.float32(np.sqrt(0.5))))


def _bdot(a, b):
    return jnp.dot(a, b, preferred_element_type=F32)


def _bdot_nt(a, b):
    return lax.dot_general(a, b, (((1,), (1,)), ((), ())), preferred_element_type=F32)


def _rope128(x, c, s):
    swapped = pltpu.roll(x, 96, axis=1) + pltpu.roll(x, 32, axis=1)
    return x * c + swapped * s


def _inproj_body(x_ref, g_ref, w_ref, gv_ref, o_ref):
    j = pl.program_id(0)
    xn = _rms(x_ref[...], g_ref[...]).astype(BF)
    y = _bdot(xn, w_ref[...].astype(BF))

    @pl.when(j == 0)
    def _():
        o_ref[0] = _gelu(y)

    @pl.when(j == 1)
    def _():
        o_ref[0] = _rms(_gelu(y), gv_ref[...])

    @pl.when(j == 2)
    def _():
        o_ref[0] = y


def _inproj(x, g, w, gv):
    return pl.pallas_call(
        _inproj_body,
        grid=(3, N_TM),
        in_specs=[
            pl.BlockSpec((TM, D_MODEL), lambda j, i: (i, 0)),
            pl.BlockSpec((1, D_MODEL), lambda j, i: (0, 0)),
            pl.BlockSpec((D_MODEL, A_WIDTH), lambda j, i: (0, j)),
            pl.BlockSpec((1, A_WIDTH), lambda j, i: (0, 0)),
        ],
        out_specs=pl.BlockSpec((1, TM, A_WIDTH), lambda j, i: (j, i, 0)),
        out_shape=jax.ShapeDtypeStruct((3, T, A_WIDTH), F32),
        compiler_params=_cparams(("arbitrary", "arbitrary")),
        name="inproj",
    )(x, g, w, gv)


def _gate_body(u_ref, v_ref, ws_ref, bb_ref, a_ref):
    is_sample = pl.program_id(0) >= N_TC_P
    row = lax.broadcasted_iota(jnp.int32, (CHUNK, CHUNK), 0)
    col = lax.broadcasted_iota(jnp.int32, (CHUNK, CHUNK), 1)
    for h in range(A_HEADS):
        w = ws_ref[h]
        w_chunk = jnp.where(col <= row, w, 0.0)
        w_single = jnp.where(col == row, jnp.broadcast_to(w[0:1, 0:1], (CHUNK, CHUNK)), 0.0)
        w_eff = jnp.where(is_sample, w_single, w_chunk).astype(BF)
        b = bb_ref[h]
        b_eff = jnp.where(is_sample, jnp.broadcast_to(b[0:1, :], (CHUNK, CHUNK)), b)
        sl = slice(h * A_HEAD_DIM, (h + 1) * A_HEAD_DIM)
        mixed = _bdot(w_eff, v_ref[0, :, sl].astype(BF)) + b_eff
        a_ref[:, sl] = (u_ref[0, :, sl] * mixed).astype(BF)


def _gate(y, w_s, b_bcast):
    return pl.pallas_call(
        _gate_body,
        grid=(N_TC,),
        in_specs=[
            pl.BlockSpec((1, TC, A_WIDTH), lambda i: (0, i, 0)),
            pl.BlockSpec((1, TC, A_WIDTH), lambda i: (1, i, 0)),
            pl.BlockSpec((A_HEADS, CHUNK, CHUNK), lambda i: (0, 0, 0)),
            pl.BlockSpec((A_HEADS, CHUNK, CHUNK), lambda i: (0, 0, 0)),
        ],
        out_specs=pl.BlockSpec((TC, A_WIDTH), lambda i: (i, 0)),
        out_shape=jax.ShapeDtypeStruct((T, A_WIDTH), BF),
        compiler_params=_cparams(("arbitrary",)),
        name="gate",
    )(y, y, w_s, b_bcast)


def _pool_body(z_ref, hist_ref, wp_ref, ps_ref, o_ref):
    b = pl.program_id(0)

    def project(pooled, g):
        out = _bdot(pooled.astype(BF), wp_ref[g].astype(BF)) * ps_ref[g:g + 1, :]
        return out.astype(BF)

    @pl.when(b < BATCH)
    def _():
        row = lax.broadcasted_iota(jnp.int32, (SEQ, B_GROUP_DIM), 0)
        for g, w in enumerate(POOL_WINDOWS):
            sl = slice(g * B_GROUP_DIM, (g + 1) * B_GROUP_DIM)
            z = z_ref[0, :, sl]
            s = z
            k = 1
            while k < w:
                s = s + jnp.where(row >= k, pltpu.roll(s, k, axis=0), 0.0)
                k *= 2
            cnt = jnp.minimum(row + 1, w).astype(F32)
            o_ref[:, sl] = project(s / cnt - z, g)

    @pl.when(b == BATCH)
    def _():
        for g, w in enumerate(POOL_WINDOWS):
            sl = slice(g * B_GROUP_DIM, (g + 1) * B_GROUP_DIM)
            z = z_ref[0, 0:T_S, sl]
            s = z
            for k in range(POOL_HIST + 1 - w, POOL_HIST):
                s = s + hist_ref[:, k * B_WIDTH + g * B_GROUP_DIM:k * B_WIDTH + (g + 1) * B_GROUP_DIM]
            o_ref[0:T_S, sl] = project(s / np.float32(w) - z, g)


def _pool(y, hist2d, w_pool, pool_scale):
    return pl.pallas_call(
        _pool_body,
        grid=(BATCH + 1,),
        in_specs=[
            pl.BlockSpec((1, SEQ, B_WIDTH), lambda b: (2, b, 0)),
            pl.BlockSpec((T_S, POOL_HIST * B_WIDTH), lambda b: (0, 0)),
            pl.BlockSpec((len(POOL_WINDOWS), B_GROUP_DIM, B_GROUP_DIM), lambda b: (0, 0, 0)),
            pl.BlockSpec((len(POOL_WINDOWS), B_GROUP_DIM), lambda b: (0, 0)),
        ],
        out_specs=pl.BlockSpec((SEQ, B_WIDTH), lambda b: (b, 0)),
        out_shape=jax.ShapeDtypeStruct((T, B_WIDTH), BF),
        compiler_params=_cparams(("arbitrary",)),
        name="pool",
    )(y, hist2d, w_pool, pool_scale)


def _proj_res_body(*refs, n_in):
    a_refs = refs[:n_in]
    w_ref, r_ref, o_ref = refs[n_in:]
    acc = r_ref[...]
    off = 0
    for a_ref in a_refs:
        k = a_ref.shape[1]
        acc = acc + _bdot(a_ref[...], w_ref[off:off + k, :].astype(BF))
        off += k
    o_ref[...] = acc


def _proj_res(acts, w, res, name):
    tn = 1024
    k_total = w.shape[0]
    return pl.pallas_call(
        functools.partial(_proj_res_body, n_in=len(acts)),
        grid=(D_MODEL // tn, N_TM),
        in_specs=[pl.BlockSpec((TM, a.shape[1]), lambda j, i: (i, 0)) for a in acts] + [
            pl.BlockSpec((k_total, tn), lambda j, i: (0, j)),
            pl.BlockSpec((TM, tn), lambda j, i: (i, j)),
        ],
        out_specs=pl.BlockSpec((TM, tn), lambda j, i: (i, j)),
        out_shape=jax.ShapeDtypeStruct((T, D_MODEL), F32),
        compiler_params=_cparams(("arbitrary", "arbitrary")),
        name=name,
    )(*acts, w, res)


def _ffn_body(x_hbm, g_ref, wg_ref, wu_ref, wd_ref, o_ref, xn_ref, sem):
    i = pl.program_id(0)
    f = pl.program_id(1)

    @pl.when(f == 0)
    def _():
        cp = pltpu.make_async_copy(x_hbm.at[pl.ds(i * TM_FFN, TM_FFN)], o_ref, sem)
        cp.start()
        cp.wait()
        xn_ref[...] = _rms(o_ref[...], g_ref[...]).astype(BF)

    xn = xn_ref[...]
    gate = _bdot(xn, wg_ref[...].astype(BF))
    up = _bdot(xn, wu_ref[...].astype(BF))
    act = (gate * jax.nn.sigmoid(gate) * up).astype(BF)
    o_ref[...] += _bdot(act, wd_ref[...].astype(BF))


def _ffn(x, g, wg, wu, wd):
    return pl.pallas_call(
        _ffn_body,
        grid=(T // TM_FFN, D_FF // TF),
        in_specs=[
            pl.BlockSpec(memory_space=pl.ANY),
            pl.BlockSpec((1, D_MODEL), lambda i, f: (0, 0)),
            pl.BlockSpec((D_MODEL, TF), lambda i, f: (0, f)),
            pl.BlockSpec((D_MODEL, TF), lambda i, f: (0, f)),
            pl.BlockSpec((TF, D_MODEL), lambda i, f: (f, 0)),
        ],
        out_specs=pl.BlockSpec((TM_FFN, D_MODEL), lambda i, f: (i, 0)),
        out_shape=jax.ShapeDtypeStruct((T, D_MODEL), F32),
        scratch_shapes=[pltpu.VMEM((TM_FFN, D_MODEL), BF), pltpu.SemaphoreType.DMA(())],
        compiler_params=_cparams(("arbitrary", "arbitrary")),
        name="ffn",
    )(x, g, wg, wu, wd)


def _dqkv_body(x_ref, g_ref, w_ref, gq_ref, gkv_ref, cos_ref, sin_ref,
               cq_ref, ckv_ref, pe_ref, lat_ref):
    xn = _rms(x_ref[...], g_ref[...]).astype(BF)
    d = _bdot(xn, w_ref[...].astype(BF))
    cq_ref[...] = _rms(d[:, :Q_LORA], gq_ref[...]).astype(BF)
    ckv = _rms(d[:, Q_LORA:Q_LORA + KV_LORA], gkv_ref[...])
    pe = _rope128(d[:, Q_LORA + KV_LORA:], cos_ref[...], sin_ref[...])
    ckv_ref[...] = ckv.astype(BF)
    pe_ref[...] = pe.astype(BF)
    lat_ref[:, :KV_LORA] = ckv
    lat_ref[:, KV_LORA:] = pe[:, :QK_ROPE]


def _dqkv(x, g, w_pad, gq, gkv, cos_t, sin_t):
    n = w_pad.shape[1]
    row = lambda i: (i, 0)
    fix = lambda i: (0, 0)
    return pl.pallas_call(
        _dqkv_body,
        grid=(N_TM,),
        in_specs=[
            pl.BlockSpec((TM, D_MODEL), row),
            pl.BlockSpec((1, D_MODEL), fix),
            pl.BlockSpec((D_MODEL, n), fix),
            pl.BlockSpec((1, Q_LORA), fix),
            pl.BlockSpec((1, KV_LORA), fix),
            pl.BlockSpec((TM, 128), row),
            pl.BlockSpec((TM, 128), row),
        ],
        out_specs=[
            pl.BlockSpec((TM, Q_LORA), row),
            pl.BlockSpec((TM, KV_LORA), row),
            pl.BlockSpec((TM, 128), row),
            pl.BlockSpec((TM, LATENT), row),
        ],
        out_shape=[
            jax.ShapeDtypeStruct((T, Q_LORA), BF),
            jax.ShapeDtypeStruct((T, KV_LORA), BF),
            jax.ShapeDtypeStruct((T, 128), BF),
            jax.ShapeDtypeStruct((T, LATENT), F32),
        ],
        compiler_params=_cparams(("arbitrary",)),
        name="dqkv",
    )(x, g, w_pad, gq, gkv, cos_t, sin_t)


HG = 4


def _qkv_body(cq_ref, ckv_ref, pe_ref, cos_ref, sin_ref, wq_ref, wk_ref, wvt_ref,
              q_ref, k_ref, vt_ref):
    q = _bdot(cq_ref[...], wq_ref[...].astype(BF)) * np.float32(ATTN_SCALE)
    ckv = ckv_ref[...]
    kn = _bdot(ckv, wk_ref[...].astype(BF))
    pe = pe_ref[...]
    c = cos_ref[...]
    s = sin_ref[...]
    for h in range(HG):
        o = h * HEAD_PAD
        q_ref[:, o:o + QK_NOPE] = q[:, o:o + QK_NOPE].astype(BF)
        q_ref[:, o + QK_NOPE:o + HEAD_PAD] = _rope128(q[:, o + QK_NOPE:o + HEAD_PAD], c, s).astype(BF)
        k_ref[:, o:o + QK_NOPE] = kn[:, h * QK_NOPE:(h + 1) * QK_NOPE].astype(BF)
        k_ref[:, o + QK_NOPE:o + HEAD_PAD] = pe
    vt_ref[...] = _bdot_nt(wvt_ref[...].astype(BF), ckv).astype(BF)


def _qkv(cq, ckv, pe, cos_t, sin_t, wq_pad, wk, wv_t):
    row = lambda j, i: (i, 0)
    col = lambda j, i: (0, j)
    out = lambda j, i: (i, j)
    return pl.pallas_call(
        _qkv_body,
        grid=(N_HEADS // HG, N_TM),
        in_specs=[
            pl.BlockSpec((TM, Q_LORA), row),
            pl.BlockSpec((TM, KV_LORA), row),
            pl.BlockSpec((TM, 128), row),
            pl.BlockSpec((TM, 128), row),
            pl.BlockSpec((TM, 128), row),
            pl.BlockSpec((Q_LORA, HG * HEAD_PAD), col),
            pl.BlockSpec((KV_LORA, HG * QK_NOPE), col),
            pl.BlockSpec((HG * V_HEAD, KV_LORA), lambda j, i: (j, 0)),
        ],
        out_specs=[
            pl.BlockSpec((TM, HG * HEAD_PAD), out),
            pl.BlockSpec((TM, HG * HEAD_PAD), out),
            pl.BlockSpec((HG * V_HEAD, TM), lambda j, i: (j, i)),
        ],
        out_shape=[
            jax.ShapeDtypeStruct((T, N_HEADS * HEAD_PAD), BF),
            jax.ShapeDtypeStruct((T, N_HEADS * HEAD_PAD), BF),
            jax.ShapeDtypeStruct((N_HEADS * V_HEAD, T), BF),
        ],
        compiler_params=_cparams(("arbitrary", "arbitrary")),
        name="qkv",
    )(cq, ckv, pe, cos_t, sin_t, wq_pad, wk, wv_t)


_PAIRS = [(qi, ki) for qi in range(N_QT) for ki in range(qi + 1)]


def _flash_body(qi_tab, ki_tab, q_ref, k_ref, vt_ref, o_ref, m_ref, l_ref, acc_ref):
    p = pl.program_id(1)
    qi = qi_tab[p]
    ki = ki_tab[p]

    @pl.when(ki == 0)
    def _():
        m_ref[...] = jnp.full(m_ref.shape, -jnp.inf, F32)
        l_ref[...] = jnp.zeros(l_ref.shape, F32)
        acc_ref[...] = jnp.zeros(acc_ref.shape, F32)

    key = lax.broadcasted_iota(jnp.int32, (TQ, TQ), 0)
    qry = lax.broadcasted_iota(jnp.int32, (TQ, TQ), 1)
    visible = key - qry <= (qi - ki) * TQ
    for h in range(N_HEADS):
        qk = slice(h * HEAD_PAD, (h + 1) * HEAD_PAD)
        vs = slice(h * V_HEAD, (h + 1) * V_HEAD)
        st = _bdot_nt(k_ref[:, qk], q_ref[:, qk])
        st = jnp.where(visible, st, -jnp.inf)
        m_prev = m_ref[h]
        m_new = jnp.maximum(m_prev, jnp.max(st, axis=0, keepdims=True))
        alpha = jnp.exp(m_prev - m_new)
        pt = jnp.exp(st - m_new)
        l_ref[h] = alpha * l_ref[h] + jnp.sum(pt, axis=0, keepdims=True)
        acc_ref[vs, :] = alpha * acc_ref[vs, :] + _bdot(vt_ref[vs, :], pt.astype(BF))
        m_ref[h] = m_new

    @pl.when(ki == qi)
    def _():
        for h in range(N_HEADS):
            vs = slice(h * V_HEAD, (h + 1) * V_HEAD)
            o_ref[:, vs] = (acc_ref[vs, :] / l_ref[h]).T.astype(BF)


def _flash(q, k, vt):
    qi_tab = jnp.asarray(np.array([p[0] for p in _PAIRS], np.int32))
    ki_tab = jnp.asarray(np.array([p[1] for p in _PAIRS], np.int32))
    grid_spec = pltpu.PrefetchScalarGridSpec(
        num_scalar_prefetch=2,
        grid=(BATCH, len(_PAIRS)),
        in_specs=[
            pl.BlockSpec((TQ, N_HEADS * HEAD_PAD), lambda b, p, qt, kt: (b * N_QT + qt[p], 0)),
            pl.BlockSpec((TQ, N_HEADS * HEAD_PAD), lambda b, p, qt, kt: (b * N_QT + kt[p], 0)),
            pl.BlockSpec((N_HEADS * V_HEAD, TQ), lambda b, p, qt, kt: (0, b * N_QT + kt[p])),
        ],
        out_specs=pl.BlockSpec((TQ, N_HEADS * V_HEAD), lambda b, p, qt, kt: (b * N_QT + qt[p], 0)),
        scratch_shapes=[
            pltpu.VMEM((N_HEADS, 1, TQ), F32),
            pltpu.VMEM((N_HEADS, 1, TQ), F32),
            pltpu.VMEM((N_HEADS * V_HEAD, TQ), F32),
        ],
    )
    return pl.pallas_call(
        _flash_body,
        grid_spec=grid_spec,
        out_shape=jax.ShapeDtypeStruct((T_P, N_HEADS * V_HEAD), BF),
        compiler_params=_cparams(("arbitrary", "arbitrary")),
        name="flash",
    )(qi_tab, ki_tab, q, k, vt)


def _qlat_body(q_ref, wk_ref, o_ref):
    o_ref[...] = _bdot_nt(q_ref[:, :QK_NOPE], wk_ref[...].astype(BF)).astype(BF)


def _qlat(q, wk):
    return pl.pallas_call(
        _qlat_body,
        grid=(N_HEADS,),
        in_specs=[
            pl.BlockSpec((T_S, HEAD_PAD), lambda h: (T_P // T_S, h)),
            pl.BlockSpec((KV_LORA, QK_NOPE), lambda h: (0, h)),
        ],
        out_specs=pl.BlockSpec((T_S, KV_LORA), lambda h: (0, h)),
        out_shape=jax.ShapeDtypeStruct((T_S, N_HEADS * KV_LORA), BF),
        compiler_params=_cparams(("arbitrary",)),
        name="qlat",
    )(q, wk)


def _decode_body(pt_ref, qlat_ref, q_ref, lat_ref, cache_hbm, o_ref,
                 buf, kb_ref, sems, m_ref, l_ref, acc_ref):
    b = pl.program_id(0)
    c = pl.program_id(1)
    step = b * N_KV_STEPS + c
    n_steps = DEC_BATCH * N_KV_STEPS
    slot = step % 2

    def page_copy(st, j, sl):
        pg = pt_ref[st * PAGES_PER_STEP + j]
        return pltpu.make_async_copy(cache_hbm.at[pg], buf.at[sl, j], sems.at[sl])

    def start_fetch(st, sl):
        for j in range(PAGES_PER_STEP):
            page_copy(st, j, sl).start()

    @pl.when(step == 0)
    def _():
        start_fetch(0, 0)

    @pl.when(step + 1 < n_steps)
    def _():
        start_fetch(step + 1, 1 - slot)

    qlat = qlat_ref[0]
    qpe = q_ref[0, :, QK_NOPE:QK_NOPE + QK_ROPE]

    @pl.when(c == 0)
    def _():
        lat = lat_ref[pl.ds(b, 1), :]
        s_self = (jnp.sum(qlat.astype(F32) * lat[:, :KV_LORA], axis=-1, keepdims=True)
                  + jnp.sum(qpe.astype(F32) * lat[:, KV_LORA:], axis=-1, keepdims=True))
        m_ref[...] = s_self
        l_ref[...] = jnp.ones(l_ref.shape, F32)
        acc_ref[...] = jnp.broadcast_to(lat[:, :KV_LORA], acc_ref.shape)

    for j in range(PAGES_PER_STEP):
        page_copy(step, j, slot).wait()

    s_parts = []
    for j in range(PAGES_PER_STEP):
        kb_ref[j] = buf[slot, j].astype(BF)
        s_parts.append(_bdot(qlat, kb_ref[j, :KV_LORA, :]) + _bdot(qpe, kb_ref[j, KV_LORA:, :]))
    s = jnp.concatenate(s_parts, axis=1)
    m_prev = m_ref[...]
    m_new = jnp.maximum(m_prev, jnp.max(s, axis=-1, keepdims=True))
    alpha = jnp.exp(m_prev - m_new)
    pexp = jnp.exp(s - m_new)
    l_ref[...] = alpha * l_ref[...] + jnp.sum(pexp, axis=-1, keepdims=True)
    pb = pexp.astype(BF)
    pv = jnp.zeros(acc_ref.shape, F32)
    for j in range(PAGES_PER_STEP):
        pv = pv + _bdot_nt(pb[:, j * PAGE_SIZE:(j + 1) * PAGE_SIZE], kb_ref[j, :KV_LORA, :])
    acc_ref[...] = alpha * acc_ref[...] + pv
    m_ref[...] = m_new

    @pl.when(c == N_KV_STEPS - 1)
    def _():
        o_ref[0] = acc_ref[...] / l_ref[...]


def _decode(page_table_flat, qlat3, q3, lat_s, cache_t):
    grid_spec = pltpu.PrefetchScalarGridSpec(
        num_scalar_prefetch=1,
        grid=(DEC_BATCH, N_KV_STEPS),
        in_specs=[
            pl.BlockSpec((1, N_HEADS, KV_LORA), lambda b, c, pt: (b, 0, 0)),
            pl.BlockSpec((1, N_HEADS, HEAD_PAD), lambda b, c, pt: (b, 0, 0)),
            pl.BlockSpec((T_S, LATENT), lambda b, c, pt: (0, 0)),
            pl.BlockSpec(memory_space=pl.ANY),
        ],
        out_specs=pl.BlockSpec((1, N_HEADS, KV_LORA), lambda b, c, pt: (b, 0, 0)),
        scratch_shapes=[
            pltpu.VMEM((2, PAGES_PER_STEP, LATENT, PAGE_SIZE), F32),
            pltpu.VMEM((PAGES_PER_STEP, LATENT, PAGE_SIZE), BF),
            pltpu.SemaphoreType.DMA((2,)),
            pltpu.VMEM((N_HEADS, 1), F32),
            pltpu.VMEM((N_HEADS, 1), F32),
            pltpu.VMEM((N_HEADS, KV_LORA), F32),
        ],
    )
    return pl.pallas_call(
        _decode_body,
        grid_spec=grid_spec,
        out_shape=jax.ShapeDtypeStruct((DEC_BATCH, N_HEADS, KV_LORA), F32),
        compiler_params=_cparams(("arbitrary", "arbitrary")),
        name="decode",
    )(page_table_flat, qlat3, q3, lat_s, cache_t)


def _ouv_body(o_ref, wv_ref, out_ref):
    out_ref[...] = _bdot(o_ref[...].astype(BF), wv_ref[...].astype(BF)).astype(BF)


def _ouv(olat2d, wv):
    return pl.pallas_call(
        _ouv_body,
        grid=(N_HEADS,),
        in_specs=[
            pl.BlockSpec((T_S, KV_LORA), lambda h: (0, h)),
            pl.BlockSpec((KV_LORA, V_HEAD), lambda h: (0, h)),
        ],
        out_specs=pl.BlockSpec((T_S, V_HEAD), lambda h: (0, h)),
        out_shape=jax.ShapeDtypeStruct((T_S, N_HEADS * V_HEAD), BF),
        compiler_params=_cparams(("arbitrary",)),
        name="ouv",
    )(olat2d, wv)


def _router_body(x_ref, g_ref, w_ref, meta_ref):
    xn = _rms(x_ref[...], g_ref[...])
    w = w_ref[...]
    xh = xn.astype(BF)
    xl = (xn - xh.astype(F32)).astype(BF)
    wh = w.astype(BF)
    wl = (w - wh.astype(F32)).astype(BF)
    logits = _bdot(xh, wh) + (_bdot(xh, wl) + _bdot(xl, wh))
    lane = lax.broadcasted_iota(jnp.int32, logits.shape, 1).astype(F32)
    lg = jnp.where(lane < N_EXPERTS, logits, -jnp.inf)
    t1 = jnp.max(lg, axis=-1, keepdims=True)
    i1 = jnp.min(jnp.where(lg == t1, lane, 128.0), axis=-1, keepdims=True)
    lg2 = jnp.where(lane == i1, -jnp.inf, lg)
    t2 = jnp.max(lg2, axis=-1, keepdims=True)
    i2 = jnp.min(jnp.where(lg2 == t2, lane, 128.0), axis=-1, keepdims=True)
    e = jnp.exp(t2 - t1)
    g1 = 1.0 / (1.0 + e)
    g2 = e / (1.0 + e)
    meta = jnp.where(lane == 0, i1, jnp.where(lane == 1, i2, jnp.where(lane == 2, g1, g2)))
    meta_ref[...] = meta


def _router(x, g, w_pad):
    return pl.pallas_call(
        _router_body,
        grid=(N_TM,),
        in_specs=[
            pl.BlockSpec((TM, D_MODEL), lambda i: (i, 0)),
            pl.BlockSpec((1, D_MODEL), lambda i: (0, 0)),
            pl.BlockSpec((D_MODEL, 128), lambda i: (0, 0)),
        ],
        out_specs=pl.BlockSpec((TM, 128), lambda i: (i, 0)),
        out_shape=jax.ShapeDtypeStruct((T, 128), F32),
        compiler_params=_cparams(("arbitrary",)),
        name="router",
    )(x, g, w_pad)


N_FF = D_FF // TF
N_CH = MOE_TM // MOE_CH


def _moe_body(te_ref, rows_ref, ybase_ref, src_ref, h_hbm, g_ref, wg_ref, wu_ref, wd_ref, y_hbm,
              acc_ref, xn_ref, wgb_ref, wub_ref, wdb_ref, zero_ref, gsems, wsem):
    t = pl.program_id(0)
    f = pl.program_id(1)
    nch = (rows_ref[t] + MOE_CH - 1) // MOE_CH
    nblk = nch // 2
    has_rem = nch % 2 == 1
    base = pl.multiple_of(ybase_ref[t], MOE_CH)

    def issue_gather(ch):
        def body(r, carry):
            tok = src_ref[base + ch * MOE_CH + r]
            pltpu.make_async_copy(h_hbm.at[pl.ds(tok, 1)],
                                  acc_ref.at[pl.ds(ch * MOE_CH + r, 1)], gsems.at[ch]).start()
            return carry
        lax.fori_loop(0, MOE_CH, body, 0)

    def wait_gather(ch):
        pltpu.make_async_copy(h_hbm.at[pl.ds(0, MOE_CH)],
                              acc_ref.at[pl.ds(ch * MOE_CH, MOE_CH)], gsems.at[ch]).wait()

    def writeback(row0, nr):
        return pltpu.make_async_copy(acc_ref.at[pl.ds(row0, nr)],
                                     y_hbm.at[pl.ds(pl.multiple_of(base + row0, MOE_CH), nr)], wsem)

    def block(ch0, nr, first, last):
        row0 = pl.multiple_of(ch0 * MOE_CH, MOE_CH)
        rs = pl.ds(row0, nr)
        if first:
            for c in range(nr // MOE_CH):
                wait_gather(ch0 + c)
            xn_ref[rs, :] = _rms(acc_ref[rs, :], g_ref[...]).astype(BF)
        xn = xn_ref[rs, :]
        gate = _bdot(xn, wgb_ref[...])
        up = _bdot(xn, wub_ref[...])
        act = (gate * jax.nn.sigmoid(gate) * up).astype(BF)
        out = _bdot(act, wdb_ref[...])
        if first:
            acc_ref[rs, :] = out
        else:
            acc_ref[rs, :] += out
        if last:
            writeback(row0, nr).start()

    def sweep(first, last):
        if first:
            lax.fori_loop(0, nch, lambda ch, c: (issue_gather(ch), c)[1], 0)
        lax.fori_loop(0, nblk, lambda b, c: (block(2 * b, MOE_BLK, first, last), c)[1], 0)

        @pl.when(has_rem)
        def _():
            block(nch - 1, MOE_CH, first, last)

        if last:
            lax.fori_loop(0, nblk, lambda b, c: (writeback(0, MOE_BLK).wait(), c)[1], 0)

            @pl.when(has_rem)
            def _():
                writeback(0, MOE_CH).wait()

    @pl.when(nch > 0)
    def _():
        wgb_ref[...] = wg_ref[0].astype(BF)
        wub_ref[...] = wu_ref[0].astype(BF)
        wdb_ref[...] = wd_ref[0].astype(BF)

        @pl.when(f == 0)
        def _():
            sweep(True, False)

        @pl.when(jnp.logical_and(f > 0, f < N_FF - 1))
        def _():
            sweep(False, False)

        @pl.when(f == N_FF - 1)
        def _():
            sweep(False, True)

    @pl.when(jnp.logical_and(t == MOE_NT - 1, f == N_FF - 1))
    def _():
        used = (base + nch * MOE_CH) // MOE_CH
        zero_ref[...] = jnp.zeros(zero_ref.shape, F32)

        def fill(c):
            return pltpu.make_async_copy(
                zero_ref, y_hbm.at[pl.ds(pl.multiple_of(c * MOE_CH, MOE_CH), MOE_CH)], wsem)

        lax.fori_loop(used, MOE_ROWS // MOE_CH, lambda c, k: (fill(c).start(), k)[1], 0)
        lax.fori_loop(used, MOE_ROWS // MOE_CH, lambda c, k: (fill(0).wait(), k)[1], 0)


def _moe(tile_expert, tile_rows, ybase, src, h, g, wg, wu, wd):
    ff = lambda t, f, nr: jnp.where(nr[t] > 0, f, N_FF - 1)
    grid_spec = pltpu.PrefetchScalarGridSpec(
        num_scalar_prefetch=4,
        grid=(MOE_NT, N_FF),
        in_specs=[
            pl.BlockSpec(memory_space=pl.ANY),
            pl.BlockSpec((1, D_MODEL), lambda t, f, te, nr, yb, sr: (0, 0)),
            pl.BlockSpec((1, D_MODEL, TF), lambda t, f, te, nr, yb, sr: (te[t], 0, ff(t, f, nr))),
            pl.BlockSpec((1, D_MODEL, TF), lambda t, f, te, nr, yb, sr: (te[t], 0, ff(t, f, nr))),
            pl.BlockSpec((1, TF, D_MODEL), lambda t, f, te, nr, yb, sr: (te[t], ff(t, f, nr), 0)),
        ],
        out_specs=pl.BlockSpec(memory_space=pl.ANY),
        scratch_shapes=[
            pltpu.VMEM((MOE_TM, D_MODEL), F32),
            pltpu.VMEM((MOE_TM, D_MODEL), BF),
            pltpu.VMEM((D_MODEL, TF), BF),
            pltpu.VMEM((D_MODEL, TF), BF),
            pltpu.VMEM((TF, D_MODEL), BF),
            pltpu.VMEM((MOE_CH, D_MODEL), F32),
            pltpu.SemaphoreType.DMA((N_CH,)),
            pltpu.SemaphoreType.DMA(()),
        ],
    )
    return pl.pallas_call(
        _moe_body,
        grid_spec=grid_spec,
        out_shape=jax.ShapeDtypeStruct((MOE_ROWS, D_MODEL), F32),
        compiler_params=_cparams(("arbitrary", "arbitrary")),
        name="moe",
    )(tile_expert, tile_rows, ybase, src, h, g, wg, wu, wd)


def _combine_body(pos_ref, h_ref, meta_ref, gf_ref, y_hbm, yp_ref, ys_ref, yb_ref, sems):
    i = pl.program_id(0)
    slot = i % 2

    def fetch(tile, sl):
        def issue(r, carry):
            tok = tile * TC + r
            for k in range(TOP_K):
                pltpu.make_async_copy(y_hbm.at[pl.ds(pos_ref[TOP_K * tok + k], 1)],
                                      yb_ref.at[sl, k, pl.ds(r, 1)], sems.at[sl]).start()
            return carry
        lax.fori_loop(0, TC, issue, 0)

    @pl.when(i == 0)
    def _():
        fetch(0, 0)

    @pl.when(i + 1 < N_TC)
    def _():
        fetch(i + 1, 1 - slot)

    for k in range(TOP_K):
        pltpu.make_async_copy(y_hbm.at[pl.ds(0, TC)], yb_ref.at[slot, k], sems.at[slot]).wait()
    meta = meta_ref[...]
    out = h_ref[...] + meta[:, 2:3] * yb_ref[slot, 0] + meta[:, 3:4] * yb_ref[slot, 1]
    out = _rms(out, gf_ref[...])

    @pl.when(i < N_TC_P)
    def _():
        yp_ref[...] = out

    @pl.when(i >= N_TC_P)
    def _():
        ys_ref[...] = out


def _combine(pos, h, meta, gf, y):
    grid_spec = pltpu.PrefetchScalarGridSpec(
        num_scalar_prefetch=1,
        grid=(N_TC,),
        in_specs=[
            pl.BlockSpec((TC, D_MODEL), lambda i, pos: (i, 0)),
            pl.BlockSpec((TC, 128), lambda i, pos: (i, 0)),
            pl.BlockSpec((1, D_MODEL), lambda i, pos: (0, 0)),
            pl.BlockSpec(memory_space=pl.ANY),
        ],
        out_specs=[
            pl.BlockSpec((TC, D_MODEL), lambda i, pos: (jnp.minimum(i, N_TC_P - 1), 0)),
            pl.BlockSpec((T_S, D_MODEL), lambda i, pos: (0, 0)),
        ],
        scratch_shapes=[
            pltpu.VMEM((2, TOP_K, TC, D_MODEL), F32),
            pltpu.SemaphoreType.DMA((2,)),
        ],
    )
    return pl.pallas_call(
        _combine_body,
        grid_spec=grid_spec,
        out_shape=[
            jax.ShapeDtypeStruct((T_P, D_MODEL), F32),
            jax.ShapeDtypeStruct((T_S, D_MODEL), F32),
        ],
        compiler_params=_cparams(("arbitrary",)),
        name="combine",
    )(pos, h, meta, gf, y)


def _routing_tables(experts):
    flat = experts.reshape(-1)
    onehot = (flat[:, None] == jnp.arange(N_EXPERTS, dtype=jnp.int32)[None, :]).astype(jnp.int32)
    csum = jnp.cumsum(onehot, axis=0)
    rank = jnp.take_along_axis(csum, flat[:, None], axis=1)[:, 0] - 1
    counts = csum[-1]
    tiles_per = (counts + MOE_TM - 1) // MOE_TM
    tile_end = jnp.cumsum(tiles_per)
    tile_start = tile_end - tiles_per
    n_used = tile_end[-1]
    tid = jnp.arange(MOE_NT, dtype=jnp.int32)
    last_tile = jnp.maximum(n_used - 1, 0)
    te = jnp.sum(tile_end[None, :] <= jnp.minimum(tid, last_tile)[:, None], axis=1).astype(jnp.int32)
    te = jnp.minimum(te, N_EXPERTS - 1)
    rows = jnp.clip(counts[te] - (tid - tile_start[te]) * MOE_TM, 0, MOE_TM)
    rows = jnp.where(tid < n_used, rows, 0).astype(jnp.int32)
    rows_pad = (rows + MOE_CH - 1) // MOE_CH * MOE_CH
    ybase = (jnp.cumsum(rows_pad) - rows_pad).astype(jnp.int32)
    pos = (ybase[tile_start[flat] + rank // MOE_TM] + rank % MOE_TM).astype(jnp.int32)
    tok = jnp.arange(T * TOP_K, dtype=jnp.int32) // TOP_K
    src = jnp.zeros((MOE_ROWS,), jnp.int32).at[pos].set(tok)
    return te, rows, ybase, src, pos


def _rope_tables():
    half = QK_ROPE // 2
    inv = 1.0 / (ROPE_THETA ** (jnp.arange(half, dtype=F32) * 2.0 / QK_ROPE))
    pos = jnp.concatenate([jnp.tile(jnp.arange(SEQ), BATCH), jnp.full((T_S,), PAST_LEN)])
    ang = pos.astype(F32)[:, None] * inv[None, :]
    cos, sin = jnp.cos(ang), jnp.sin(ang)
    zero = jnp.zeros((T, 128 - QK_ROPE), F32)
    return (jnp.concatenate([cos, cos, zero], axis=1),
            jnp.concatenate([-sin, sin, zero], axis=1))


def kernel(x_prompt, x_sample, state_pool, cache_mla, page_table, norm_mix, norm_ffn, w_in, g_v, w_s, b_s, w_pool, pool_scale, w_o_mix, w_ffn_gate, w_ffn_up, w_ffn_down, w_dqkv, g_q, g_kv, w_uq, w_uk, w_uv, w_o_attn, w_router, w_exp_gate, w_exp_up, w_exp_down, norm_final):
    x = jnp.concatenate([x_prompt.reshape(T_P, D_MODEL), x_sample.reshape(T_S, D_MODEL)], axis=0)

    y = _inproj(x, norm_mix[0:1], w_in[0], g_v)
    b_bcast = jnp.broadcast_to(b_s[0][:, :, None], (A_HEADS, CHUNK, CHUNK))
    a_out = _gate(y, w_s[0], b_bcast)
    hist2d = state_pool[0].reshape(T_S, POOL_HIST * B_WIDTH)
    b_out = _pool(y, hist2d, w_pool[0], pool_scale[0])
    h = _proj_res([a_out, b_out], w_o_mix[0], x, "mix_out")
    h = _ffn(h, norm_ffn[0:1], w_ffn_gate[0], w_ffn_up[0], w_ffn_down[0])

    pool_state_prompt = jnp.stack(
        [y[2, (b + 1) * SEQ - POOL_HIST:(b + 1) * SEQ] for b in range(BATCH)])[None]
    z_s = y[2, T_P:]
    pool_state_sample = jnp.concatenate([state_pool[0][:, 1:], z_s[:, None, :]], axis=1)[None]
    chunk_v_sample = y[1, T_P:].reshape(1, T_S, 1, A_WIDTH)

    cos_t, sin_t = _rope_tables()
    w_dqkv_pad = jnp.pad(w_dqkv[0], ((0, 0), (0, 128 - QK_ROPE)))
    cq, ckv, pe, lat = _dqkv(h, norm_mix[1:2], w_dqkv_pad, g_q, g_kv, cos_t, sin_t)
    wq_pad = jnp.pad(w_uq[0].reshape(Q_LORA, N_HEADS, QK_NOPE + QK_ROPE),
                     ((0, 0), (0, 0), (0, HEAD_PAD - QK_NOPE - QK_ROPE))).reshape(Q_LORA, N_HEADS * HEAD_PAD)
    wk2d = w_uk[0].reshape(KV_LORA, N_HEADS * QK_NOPE)
    wv2d = w_uv[0].reshape(KV_LORA, N_HEADS * V_HEAD)
    q, k, vt = _qkv(cq, ckv, pe, cos_t, sin_t, wq_pad, wk2d, wv2d.T)
    attn = _flash(q, k, vt)

    qlat = _qlat(q, wk2d)
    cache_t = jnp.swapaxes(cache_mla, 2, 3).reshape(-1, LATENT, PAGE_SIZE)
    olat = _decode(page_table.reshape(-1),
                   qlat.reshape(T_S, N_HEADS, KV_LORA),
                   q[T_P:].reshape(T_S, N_HEADS, HEAD_PAD),
                   lat[T_P:],
                   cache_t)
    attn_s = _ouv(olat.reshape(T_S, N_HEADS * KV_LORA), wv2d)
    attn = jnp.concatenate([attn, attn_s], axis=0)
    h = _proj_res([attn], w_o_attn[0], h, "attn_out")

    mla_rows_prompt = lat[:T_P].reshape(1, BATCH, SEQ, LATENT)
    mla_rows_sample = lat[T_P:].reshape(1, T_S, 1, LATENT)

    w_router_pad = jnp.pad(w_router[0], ((0, 0), (0, 128 - N_EXPERTS)))
    meta = _router(h, norm_ffn[1:2], w_router_pad)
    experts = meta[:, :TOP_K].astype(jnp.int32)
    te, rows, ybase, src, pos = _routing_tables(experts)
    y_sorted = _moe(te, rows, ybase, src, h, norm_ffn[1:2],
                    w_exp_gate[0], w_exp_up[0], w_exp_down[0])
    y_p, y_s = _combine(pos, h, meta, norm_final.reshape(1, D_MODEL), y_sorted)

    return (y_p.reshape(BATCH, SEQ, D_MODEL), y_s.reshape(T_S, 1, D_MODEL),
            pool_state_prompt, pool_state_sample, chunk_v_sample,
            mla_rows_prompt, mla_rows_sample)
```

```python
import functools

import numpy as np
import jax
import jax.numpy as jnp
from jax import lax
from jax.experimental import pallas as pl
from jax.experimental.pallas import tpu as pltpu

D_MODEL = 2048
BATCH = 4
SEQ = 2048
DEC_BATCH = 128
PAST_LEN = 8192
PAGE_SIZE = 128
N_PAGES = PAST_LEN // PAGE_SIZE
A_WIDTH = 1024
A_HEADS = 8
A_HEAD_DIM = 128
CHUNK = 128
B_WIDTH = 1024
POOL_WINDOWS = (2, 4, 8, 16)
B_GROUP_DIM = 256
POOL_HIST = 15
N_HEADS = 16
QK_NOPE = 128
QK_ROPE = 64
V_HEAD = 128
Q_LORA = 512
KV_LORA = 512
LATENT = KV_LORA + QK_ROPE
ROPE_THETA = 10000.0
ATTN_SCALE = (QK_NOPE + QK_ROPE) ** -0.5
D_FF = 7168
N_EXPERTS = 8
TOP_K = 2
EPS = 1e-6

T_P = BATCH * SEQ
T_S = DEC_BATCH
T = T_P + T_S

BF = jnp.bfloat16
F32 = jnp.float32

VMEM_LIMIT_BYTES = 56 * 1024 * 1024

TM = 640
N_TM = T // TM
TM_FFN = 1040
TF = 256
TF_FFN = 512
HEAD_PAD = 256
TQ = 512
N_QT = SEQ // TQ
PAGES_PER_STEP = 32
N_KV_STEPS = N_PAGES // PAGES_PER_STEP
KV_SLOTS = 3
MOE_CH = 128
MOE_BLOCKS = (4, 2, 1)
MOE_TM = 18 * MOE_CH
DMA_UNROLL = 8
MOE_NT = (T * TOP_K) // MOE_TM + N_EXPERTS
MOE_ROWS = -(-(T * TOP_K + MOE_NT * (MOE_CH - 1)) // MOE_CH) * MOE_CH
TC = 128
N_TC = T // TC
N_TC_P = T_P // TC


def _cparams(sem):
    return pltpu.CompilerParams(dimension_semantics=sem, vmem_limit_bytes=VMEM_LIMIT_BYTES)


def _rms(x, g):
    return x * lax.rsqrt(jnp.mean(x * x, axis=-1, keepdims=True) + EPS) * g


def _gelu(x):
    return 0.5 * x * (1.0 + lax.erf(x * np.float32(np.sqrt(0.5))))


def _bdot(a, b):
    return jnp.dot(a, b, preferred_element_type=F32)


def _bdot_nt(a, b):
    return lax.dot_general(a, b, (((1,), (1,)), ((), ())), preferred_element_type=F32)


def _rope128(x, c, s):
    swapped = pltpu.roll(x, 96, axis=1) + pltpu.roll(x, 32, axis=1)
    return x * c + swapped * s


def _inproj_body(x_ref, g_ref, w_ref, gv_ref, o_ref):
    j = pl.program_id(0)
    xn = _rms(x_ref[...], g_ref[...]).astype(BF)
    y = _bdot(xn, w_ref[...].astype(BF))

    @pl.when(j == 0)
    def _():
        o_ref[0] = _gelu(y)

    @pl.when(j == 1)
    def _():
        o_ref[0] = _rms(_gelu(y), gv_ref[...])

    @pl.when(j == 2)
    def _():
        o_ref[0] = y


def _inproj(x, g, w, gv):
    return pl.pallas_call(
        _inproj_body,
        grid=(3, N_TM),
        in_specs=[
            pl.BlockSpec((TM, D_MODEL), lambda j, i: (i, 0)),
            pl.BlockSpec((1, D_MODEL), lambda j, i: (0, 0)),
            pl.BlockSpec((D_MODEL, A_WIDTH), lambda j, i: (0, j)),
            pl.BlockSpec((1, A_WIDTH), lambda j, i: (0, 0)),
        ],
        out_specs=pl.BlockSpec((1, TM, A_WIDTH), lambda j, i: (j, i, 0)),
        out_shape=jax.ShapeDtypeStruct((3, T, A_WIDTH), F32),
        compiler_params=_cparams(("arbitrary", "arbitrary")),
        name="inproj",
    )(x, g, w, gv)


def _gate_body(u_ref, v_ref, ws_ref, bb_ref, a_ref):
    is_sample = pl.program_id(0) >= N_TC_P
    row = lax.broadcasted_iota(jnp.int32, (CHUNK, CHUNK), 0)
    col = lax.broadcasted_iota(jnp.int32, (CHUNK, CHUNK), 1)
    for h in range(A_HEADS):
        w = ws_ref[h]
        w_chunk = jnp.where(col <= row, w, 0.0)
        w_single = jnp.where(col == row, jnp.broadcast_to(w[0:1, 0:1], (CHUNK, CHUNK)), 0.0)
        w_eff = jnp.where(is_sample, w_single, w_chunk).astype(BF)
        b = bb_ref[h]
        b_eff = jnp.where(is_sample, jnp.broadcast_to(b[0:1, :], (CHUNK, CHUNK)), b)
        sl = slice(h * A_HEAD_DIM, (h + 1) * A_HEAD_DIM)
        mixed = _bdot(w_eff, v_ref[0, :, sl].astype(BF)) + b_eff
        a_ref[:, sl] = (u_ref[0, :, sl] * mixed).astype(BF)


def _gate(y, w_s, b_bcast):
    return pl.pallas_call(
        _gate_body,
        grid=(N_TC,),
        in_specs=[
            pl.BlockSpec((1, TC, A_WIDTH), lambda i: (0, i, 0)),
            pl.BlockSpec((1, TC, A_WIDTH), lambda i: (1, i, 0)),
            pl.BlockSpec((A_HEADS, CHUNK, CHUNK), lambda i: (0, 0, 0)),
            pl.BlockSpec((A_HEADS, CHUNK, CHUNK), lambda i: (0, 0, 0)),
        ],
        out_specs=pl.BlockSpec((TC, A_WIDTH), lambda i: (i, 0)),
        out_shape=jax.ShapeDtypeStruct((T, A_WIDTH), BF),
        compiler_params=_cparams(("arbitrary",)),
        name="gate",
    )(y, y, w_s, b_bcast)


def _pool_body(z_ref, hist_ref, wp_ref, ps_ref, o_ref):
    b = pl.program_id(0)

    def project(pooled, g):
        out = _bdot(pooled.astype(BF), wp_ref[g].astype(BF)) * ps_ref[g:g + 1, :]
        return out.astype(BF)

    @pl.when(b < BATCH)
    def _():
        row = lax.broadcasted_iota(jnp.int32, (SEQ, B_GROUP_DIM), 0)
        for g, w in enumerate(POOL_WINDOWS):
            sl = slice(g * B_GROUP_DIM, (g + 1) * B_GROUP_DIM)
            z = z_ref[0, :, sl]
            s = z
            k = 1
            while k < w:
                s = s + jnp.where(row >= k, pltpu.roll(s, k, axis=0), 0.0)
                k *= 2
            cnt = jnp.minimum(row + 1, w).astype(F32)
            o_ref[:, sl] = project(s / cnt - z, g)

    @pl.when(b == BATCH)
    def _():
        for g, w in enumerate(POOL_WINDOWS):
            sl = slice(g * B_GROUP_DIM, (g + 1) * B_GROUP_DIM)
            z = z_ref[0, 0:T_S, sl]
            s = z
            for k in range(POOL_HIST + 1 - w, POOL_HIST):
                s = s + hist_ref[:, k * B_WIDTH + g * B_GROUP_DIM:k * B_WIDTH + (g + 1) * B_GROUP_DIM]
            o_ref[0:T_S, sl] = project(s / np.float32(w) - z, g)


def _pool(y, hist2d, w_pool, pool_scale):
    return pl.pallas_call(
        _pool_body,
        grid=(BATCH + 1,),
        in_specs=[
            pl.BlockSpec((1, SEQ, B_WIDTH), lambda b: (2, b, 0)),
            pl.BlockSpec((T_S, POOL_HIST * B_WIDTH), lambda b: (0, 0)),
            pl.BlockSpec((len(POOL_WINDOWS), B_GROUP_DIM, B_GROUP_DIM), lambda b: (0, 0, 0)),
            pl.BlockSpec((len(POOL_WINDOWS), B_GROUP_DIM), lambda b: (0, 0)),
        ],
        out_specs=pl.BlockSpec((SEQ, B_WIDTH), lambda b: (b, 0)),
        out_shape=jax.ShapeDtypeStruct((T, B_WIDTH), BF),
        compiler_params=_cparams(("arbitrary",)),
        name="pool",
    )(y, hist2d, w_pool, pool_scale)


def _proj_res_body(*refs, n_in):
    a_refs = refs[:n_in]
    w_ref, r_ref, o_ref = refs[n_in:]
    acc = r_ref[...]
    off = 0
    for a_ref in a_refs:
        k = a_ref.shape[1]
        acc = acc + _bdot(a_ref[...], w_ref[off:off + k, :].astype(BF))
        off += k
    o_ref[...] = acc


def _proj_res(acts, w, res, name):
    tn = 1024
    k_total = w.shape[0]
    return pl.pallas_call(
        functools.partial(_proj_res_body, n_in=len(acts)),
        grid=(D_MODEL // tn, N_TM),
        in_specs=[pl.BlockSpec((TM, a.shape[1]), lambda j, i: (i, 0)) for a in acts] + [
            pl.BlockSpec((k_total, tn), lambda j, i: (0, j)),
            pl.BlockSpec((TM, tn), lambda j, i: (i, j)),
        ],
        out_specs=pl.BlockSpec((TM, tn), lambda j, i: (i, j)),
        out_shape=jax.ShapeDtypeStruct((T, D_MODEL), F32),
        compiler_params=_cparams(("arbitrary", "arbitrary")),
        name=name,
    )(*acts, w, res)


def _ffn_body(x_hbm, g_ref, wg_ref, wu_ref, wd_ref, o_hbm, acc_ref, xn_ref, sem):
    i = pl.program_id(0)
    f = pl.program_id(1)
    rows = pl.ds(i * TM_FFN, TM_FFN)

    @pl.when(f == 0)
    def _():
        cp = pltpu.make_async_copy(x_hbm.at[rows], acc_ref, sem)
        cp.start()
        cp.wait()
        xn_ref[...] = _rms(acc_ref[...], g_ref[...]).astype(BF)

    xn = xn_ref[...]
    gate = _bdot(xn, wg_ref[...].astype(BF))
    up = _bdot(xn, wu_ref[...].astype(BF))
    act = (gate * jax.nn.sigmoid(gate) * up).astype(BF)
    acc_ref[...] += _bdot(act, wd_ref[...].astype(BF))

    @pl.when(f == pl.num_programs(1) - 1)
    def _():
        cp = pltpu.make_async_copy(acc_ref, o_hbm.at[rows], sem)
        cp.start()
        cp.wait()


def _ffn(x, g, wg, wu, wd):
    return pl.pallas_call(
        _ffn_body,
        grid=(T // TM_FFN, D_FF // TF_FFN),
        in_specs=[
            pl.BlockSpec(memory_space=pl.ANY),
            pl.BlockSpec((1, D_MODEL), lambda i, f: (0, 0)),
            pl.BlockSpec((D_MODEL, TF_FFN), lambda i, f: (0, f)),
            pl.BlockSpec((D_MODEL, TF_FFN), lambda i, f: (0, f)),
            pl.BlockSpec((TF_FFN, D_MODEL), lambda i, f: (f, 0)),
        ],
        out_specs=pl.BlockSpec(memory_space=pl.ANY),
        out_shape=jax.ShapeDtypeStruct((T, D_MODEL), F32),
        scratch_shapes=[pltpu.VMEM((TM_FFN, D_MODEL), F32),
                        pltpu.VMEM((TM_FFN, D_MODEL), BF),
                        pltpu.SemaphoreType.DMA(())],
        compiler_params=_cparams(("arbitrary", "arbitrary")),
        name="ffn",
    )(x, g, wg, wu, wd)


def _dqkv_body(x_ref, g_ref, w_ref, gq_ref, gkv_ref, cos_ref, sin_ref,
               cq_ref, ckv_ref, pe_ref, lat_ref):
    xn = _rms(x_ref[...], g_ref[...]).astype(BF)
    d = _bdot(xn, w_ref[...].astype(BF))
    cq_ref[...] = _rms(d[:, :Q_LORA], gq_ref[...]).astype(BF)
    ckv = _rms(d[:, Q_LORA:Q_LORA + KV_LORA], gkv_ref[...])
    pe = _rope128(d[:, Q_LORA + KV_LORA:], cos_ref[...], sin_ref[...])
    ckv_ref[...] = ckv.astype(BF)
    pe_ref[...] = pe.astype(BF)
    lat_ref[:, :KV_LORA] = ckv
    lat_ref[:, KV_LORA:] = pe[:, :QK_ROPE]


def _dqkv(x, g, w_pad, gq, gkv, cos_t, sin_t):
    n = w_pad.shape[1]
    row = lambda i: (i, 0)
    fix = lambda i: (0, 0)
    return pl.pallas_call(
        _dqkv_body,
        grid=(N_TM,),
        in_specs=[
            pl.BlockSpec((TM, D_MODEL), row),
            pl.BlockSpec((1, D_MODEL), fix),
            pl.BlockSpec((D_MODEL, n), fix),
            pl.BlockSpec((1, Q_LORA), fix),
            pl.BlockSpec((1, KV_LORA), fix),
            pl.BlockSpec((TM, 128), row),
            pl.BlockSpec((TM, 128), row),
        ],
        out_specs=[
            pl.BlockSpec((TM, Q_LORA), row),
            pl.BlockSpec((TM, KV_LORA), row),
            pl.BlockSpec((TM, 128), row),
            pl.BlockSpec((TM, LATENT), row),
        ],
        out_shape=[
            jax.ShapeDtypeStruct((T, Q_LORA), BF),
            jax.ShapeDtypeStruct((T, KV_LORA), BF),
            jax.ShapeDtypeStruct((T, 128), BF),
            jax.ShapeDtypeStruct((T, LATENT), F32),
        ],
        compiler_params=_cparams(("arbitrary",)),
        name="dqkv",
    )(x, g, w_pad, gq, gkv, cos_t, sin_t)


HG = 4


def _qkv_body(cq_ref, ckv_ref, pe_ref, cos_ref, sin_ref, wq_ref, wk_ref, wvt_ref,
              q_ref, k_ref, vt_ref):
    q = _bdot(cq_ref[...], wq_ref[...].astype(BF)) * np.float32(ATTN_SCALE)
    ckv = ckv_ref[...]
    kn = _bdot(ckv, wk_ref[...].astype(BF))
    pe = pe_ref[...]
    c = cos_ref[...]
    s = sin_ref[...]
    for h in range(HG):
        o = h * HEAD_PAD
        q_ref[:, o:o + QK_NOPE] = q[:, o:o + QK_NOPE].astype(BF)
        q_ref[:, o + QK_NOPE:o + HEAD_PAD] = _rope128(q[:, o + QK_NOPE:o + HEAD_PAD], c, s).astype(BF)
        k_ref[:, o:o + QK_NOPE] = kn[:, h * QK_NOPE:(h + 1) * QK_NOPE].astype(BF)
        k_ref[:, o + QK_NOPE:o + HEAD_PAD] = pe
    vt_ref[...] = _bdot_nt(wvt_ref[...].astype(BF), ckv).astype(BF)


def _qkv(cq, ckv, pe, cos_t, sin_t, wq_pad, wk, wv_t):
    row = lambda j, i: (i, 0)
    col = lambda j, i: (0, j)
    out = lambda j, i: (i, j)
    return pl.pallas_call(
        _qkv_body,
        grid=(N_HEADS // HG, N_TM),
        in_specs=[
            pl.BlockSpec((TM, Q_LORA), row),
            pl.BlockSpec((TM, KV_LORA), row),
            pl.BlockSpec((TM, 128), row),
            pl.BlockSpec((TM, 128), row),
            pl.BlockSpec((TM, 128), row),
            pl.BlockSpec((Q_LORA, HG * HEAD_PAD), col),
            pl.BlockSpec((KV_LORA, HG * QK_NOPE), col),
            pl.BlockSpec((HG * V_HEAD, KV_LORA), lambda j, i: (j, 0)),
        ],
        out_specs=[
            pl.BlockSpec((TM, HG * HEAD_PAD), out),
            pl.BlockSpec((TM, HG * HEAD_PAD), out),
            pl.BlockSpec((HG * V_HEAD, TM), lambda j, i: (j, i)),
        ],
        out_shape=[
            jax.ShapeDtypeStruct((T, N_HEADS * HEAD_PAD), BF),
            jax.ShapeDtypeStruct((T, N_HEADS * HEAD_PAD), BF),
            jax.ShapeDtypeStruct((N_HEADS * V_HEAD, T), BF),
        ],
        compiler_params=_cparams(("arbitrary", "arbitrary")),
        name="qkv",
    )(cq, ckv, pe, cos_t, sin_t, wq_pad, wk, wv_t)


_PAIRS = [(qi, ki) for qi in range(N_QT) for ki in range(qi + 1)]


def _flash_body(qi_tab, ki_tab, q_ref, k_ref, vt_ref, o_ref, m_ref, l_ref, acc_ref):
    p = pl.program_id(1)
    qi = qi_tab[p]
    ki = ki_tab[p]

    @pl.when(ki == 0)
    def _():
        m_ref[...] = jnp.full(m_ref.shape, -jnp.inf, F32)
        l_ref[...] = jnp.zeros(l_ref.shape, F32)
        acc_ref[...] = jnp.zeros(acc_ref.shape, F32)

    def all_heads(on_diagonal):
        if on_diagonal:
            key = lax.broadcasted_iota(jnp.int32, (TQ, TQ), 0)
            qry = lax.broadcasted_iota(jnp.int32, (TQ, TQ), 1)
            visible = key <= qry
        for h in range(N_HEADS):
            qk = slice(h * HEAD_PAD, (h + 1) * HEAD_PAD)
            vs = slice(h * V_HEAD, (h + 1) * V_HEAD)
            st = _bdot_nt(k_ref[:, qk], q_ref[:, qk])
            if on_diagonal:
                st = jnp.where(visible, st, -jnp.inf)
            m_prev = m_ref[h]
            m_new = jnp.maximum(m_prev, jnp.max(st, axis=0, keepdims=True))
            alpha = jnp.exp(m_prev - m_new)
            pt = jnp.exp(st - m_new)
            l_ref[h] = alpha * l_ref[h] + jnp.sum(pt, axis=0, keepdims=True)
            acc_ref[vs, :] = alpha * acc_ref[vs, :] + _bdot(vt_ref[vs, :], pt.astype(BF))
            m_ref[h] = m_new

    @pl.when(ki < qi)
    def _():
        all_heads(False)

    @pl.when(ki == qi)
    def _():
        all_heads(True)
        for h in range(N_HEADS):
            vs = slice(h * V_HEAD, (h + 1) * V_HEAD)
            o_ref[:, vs] = (acc_ref[vs, :] / l_ref[h]).T.astype(BF)


def _flash(q, k, vt):
    qi_tab = jnp.asarray(np.array([p[0] for p in _PAIRS], np.int32))
    ki_tab = jnp.asarray(np.array([p[1] for p in _PAIRS], np.int32))
    grid_spec = pltpu.PrefetchScalarGridSpec(
        num_scalar_prefetch=2,
        grid=(BATCH, len(_PAIRS)),
        in_specs=[
            pl.BlockSpec((TQ, N_HEADS * HEAD_PAD), lambda b, p, qt, kt: (b * N_QT + qt[p], 0)),
            pl.BlockSpec((TQ, N_HEADS * HEAD_PAD), lambda b, p, qt, kt: (b * N_QT + kt[p], 0)),
            pl.BlockSpec((N_HEADS * V_HEAD, TQ), lambda b, p, qt, kt: (0, b * N_QT + kt[p])),
        ],
        out_specs=pl.BlockSpec((TQ, N_HEADS * V_HEAD), lambda b, p, qt, kt: (b * N_QT + qt[p], 0)),
        scratch_shapes=[
            pltpu.VMEM((N_HEADS, 1, TQ), F32),
            pltpu.VMEM((N_HEADS, 1, TQ), F32),
            pltpu.VMEM((N_HEADS * V_HEAD, TQ), F32),
        ],
    )
    return pl.pallas_call(
        _flash_body,
        grid_spec=grid_spec,
        out_shape=jax.ShapeDtypeStruct((T_P, N_HEADS * V_HEAD), BF),
        compiler_params=_cparams(("arbitrary", "arbitrary")),
        name="flash",
    )(qi_tab, ki_tab, q, k, vt)


def _qlat_body(q_ref, wk_ref, o_ref):
    o_ref[...] = _bdot_nt(q_ref[:, :QK_NOPE], wk_ref[...].astype(BF)).astype(BF)


def _qlat(q, wk):
    return pl.pallas_call(
        _qlat_body,
        grid=(N_HEADS,),
        in_specs=[
            pl.BlockSpec((T_S, HEAD_PAD), lambda h: (T_P // T_S, h)),
            pl.BlockSpec((KV_LORA, QK_NOPE), lambda h: (0, h)),
        ],
        out_specs=pl.BlockSpec((T_S, KV_LORA), lambda h: (0, h)),
        out_shape=jax.ShapeDtypeStruct((T_S, N_HEADS * KV_LORA), BF),
        compiler_params=_cparams(("arbitrary",)),
        name="qlat",
    )(q, wk)


def _decode_body(pt_ref, qlat_ref, q_ref, lat_ref, cache_hbm, o_ref,
                 buf, kb_ref, sems, m_ref, l_ref, acc_ref):
    b = pl.program_id(0)
    c = pl.program_id(1)
    step = b * N_KV_STEPS + c
    n_steps = DEC_BATCH * N_KV_STEPS
    slot = step % KV_SLOTS

    def page_copy(st, j, sl):
        pg = pt_ref[st * PAGES_PER_STEP + j]
        return pltpu.make_async_copy(cache_hbm.at[pg], buf.at[sl, j], sems.at[sl])

    def start_fetch(st, sl):
        for j in range(PAGES_PER_STEP):
            page_copy(st, j, sl).start()

    @pl.when(step == 0)
    def _():
        for s0 in range(KV_SLOTS - 1):
            start_fetch(s0, s0)

    ahead = step + (KV_SLOTS - 1)

    @pl.when(ahead < n_steps)
    def _():
        start_fetch(ahead, ahead % KV_SLOTS)

    qlat = qlat_ref[0]
    qpe = q_ref[0, :, QK_NOPE:QK_NOPE + QK_ROPE]

    @pl.when(c == 0)
    def _():
        lat = lat_ref[pl.ds(b, 1), :]
        s_self = (jnp.sum(qlat.astype(F32) * lat[:, :KV_LORA], axis=-1, keepdims=True)
                  + jnp.sum(qpe.astype(F32) * lat[:, KV_LORA:], axis=-1, keepdims=True))
        m_ref[...] = s_self
        l_ref[...] = jnp.ones(l_ref.shape, F32)
        acc_ref[...] = jnp.broadcast_to(lat[:, :KV_LORA], acc_ref.shape)

    for j in range(PAGES_PER_STEP):
        page_copy(step, j, slot).wait()

    s_parts = []
    for j in range(PAGES_PER_STEP):
        kb_ref[j] = buf[slot, j].astype(BF)
        s_parts.append(_bdot(qlat, kb_ref[j, :KV_LORA, :]) + _bdot(qpe, kb_ref[j, KV_LORA:, :]))
    s = jnp.concatenate(s_parts, axis=1)
    m_prev = m_ref[...]
    m_new = jnp.maximum(m_prev, jnp.max(s, axis=-1, keepdims=True))
    alpha = jnp.exp(m_prev - m_new)
    pexp = jnp.exp(s - m_new)
    l_ref[...] = alpha * l_ref[...] + jnp.sum(pexp, axis=-1, keepdims=True)
    pb = pexp.astype(BF)
    pv = jnp.zeros(acc_ref.shape, F32)
    for j in range(PAGES_PER_STEP):
        pv = pv + _bdot_nt(pb[:, j * PAGE_SIZE:(j + 1) * PAGE_SIZE], kb_ref[j, :KV_LORA, :])
    acc_ref[...] = alpha * acc_ref[...] + pv
    m_ref[...] = m_new

    @pl.when(c == N_KV_STEPS - 1)
    def _():
        o_ref[0] = acc_ref[...] / l_ref[...]


def _decode(page_table_flat, qlat3, q3, lat_s, cache_t):
    grid_spec = pltpu.PrefetchScalarGridSpec(
        num_scalar_prefetch=1,
        grid=(DEC_BATCH, N_KV_STEPS),
        in_specs=[
            pl.BlockSpec((1, N_HEADS, KV_LORA), lambda b, c, pt: (b, 0, 0)),
            pl.BlockSpec((1, N_HEADS, HEAD_PAD), lambda b, c, pt: (b, 0, 0)),
            pl.BlockSpec((T_S, LATENT), lambda b, c, pt: (0, 0)),
            pl.BlockSpec(memory_space=pl.ANY),
        ],
        out_specs=pl.BlockSpec((1, N_HEADS, KV_LORA), lambda b, c, pt: (b, 0, 0)),
        scratch_shapes=[
            pltpu.VMEM((KV_SLOTS, PAGES_PER_STEP, LATENT, PAGE_SIZE), F32),
            pltpu.VMEM((PAGES_PER_STEP, LATENT, PAGE_SIZE), BF),
            pltpu.SemaphoreType.DMA((KV_SLOTS,)),
            pltpu.VMEM((N_HEADS, 1), F32),
            pltpu.VMEM((N_HEADS, 1), F32),
            pltpu.VMEM((N_HEADS, KV_LORA), F32),
        ],
    )
    return pl.pallas_call(
        _decode_body,
        grid_spec=grid_spec,
        out_shape=jax.ShapeDtypeStruct((DEC_BATCH, N_HEADS, KV_LORA), F32),
        compiler_params=_cparams(("arbitrary", "arbitrary")),
        name="decode",
    )(page_table_flat, qlat3, q3, lat_s, cache_t)


def _ouv_body(o_ref, wv_ref, out_ref):
    out_ref[...] = _bdot(o_ref[...].astype(BF), wv_ref[...].astype(BF)).astype(BF)


def _ouv(olat2d, wv):
    return pl.pallas_call(
        _ouv_body,
        grid=(N_HEADS,),
        in_specs=[
            pl.BlockSpec((T_S, KV_LORA), lambda h: (0, h)),
            pl.BlockSpec((KV_LORA, V_HEAD), lambda h: (0, h)),
        ],
        out_specs=pl.BlockSpec((T_S, V_HEAD), lambda h: (0, h)),
        out_shape=jax.ShapeDtypeStruct((T_S, N_HEADS * V_HEAD), BF),
        compiler_params=_cparams(("arbitrary",)),
        name="ouv",
    )(olat2d, wv)


def _router_body(x_ref, g_ref, w_ref, meta_ref):
    xn = _rms(x_ref[...], g_ref[...])
    w = w_ref[...]
    xh = xn.astype(BF)
    xl = (xn - xh.astype(F32)).astype(BF)
    wh = w.astype(BF)
    wl = (w - wh.astype(F32)).astype(BF)
    logits = _bdot(xh, wh) + (_bdot(xh, wl) + _bdot(xl, wh))
    lane = lax.broadcasted_iota(jnp.int32, logits.shape, 1).astype(F32)
    lg = jnp.where(lane < N_EXPERTS, logits, -jnp.inf)
    t1 = jnp.max(lg, axis=-1, keepdims=True)
    i1 = jnp.min(jnp.where(lg == t1, lane, 128.0), axis=-1, keepdims=True)
    lg2 = jnp.where(lane == i1, -jnp.inf, lg)
    t2 = jnp.max(lg2, axis=-1, keepdims=True)
    i2 = jnp.min(jnp.where(lg2 == t2, lane, 128.0), axis=-1, keepdims=True)
    e = jnp.exp(t2 - t1)
    g1 = 1.0 / (1.0 + e)
    g2 = e / (1.0 + e)
    meta = jnp.where(lane == 0, i1, jnp.where(lane == 1, i2, jnp.where(lane == 2, g1, g2)))
    meta_ref[...] = meta


def _router(x, g, w_pad):
    return pl.pallas_call(
        _router_body,
        grid=(N_TM,),
        in_specs=[
            pl.BlockSpec((TM, D_MODEL), lambda i: (i, 0)),
            pl.BlockSpec((1, D_MODEL), lambda i: (0, 0)),
            pl.BlockSpec((D_MODEL, 128), lambda i: (0, 0)),
        ],
        out_specs=pl.BlockSpec((TM, 128), lambda i: (i, 0)),
        out_shape=jax.ShapeDtypeStruct((T, 128), F32),
        compiler_params=_cparams(("arbitrary",)),
        name="router",
    )(x, g, w_pad)


N_FF = D_FF // TF
N_CH = MOE_TM // MOE_CH


def _moe_body(te_ref, rows_ref, ybase_ref, src_ref, h_hbm, g_ref, wg_ref, wu_ref, wd_ref, y_hbm,
              acc_ref, xn_ref, wgb_ref, wub_ref, wdb_ref, zero_ref, gsems, wsem):
    t = pl.program_id(0)
    f = pl.program_id(1)
    nch = (rows_ref[t] + MOE_CH - 1) // MOE_CH
    big = MOE_BLOCKS[0]
    nbig = nch // big
    base = pl.multiple_of(ybase_ref[t], MOE_CH)

    def remainders():
        out = []
        ch0 = nbig * big
        left = nch - ch0
        for size in MOE_BLOCKS[1:]:
            present = left >= size
            out.append((ch0, size, present))
            ch0 = ch0 + jnp.where(present, size, 0)
            left = left - jnp.where(present, size, 0)
        return out

    def issue_gather(ch):
        def body(r0, carry):
            for j in range(DMA_UNROLL):
                r = ch * MOE_CH + r0 * DMA_UNROLL + j
                pltpu.make_async_copy(h_hbm.at[pl.ds(src_ref[base + r], 1)],
                                      acc_ref.at[pl.ds(r, 1)], gsems.at[ch]).start()
            return carry
        lax.fori_loop(0, MOE_CH // DMA_UNROLL, body, 0)

    def wait_gather(ch):
        pltpu.make_async_copy(h_hbm.at[pl.ds(0, MOE_CH)],
                              acc_ref.at[pl.ds(ch * MOE_CH, MOE_CH)], gsems.at[ch]).wait()

    def writeback(row0, nr):
        return pltpu.make_async_copy(acc_ref.at[pl.ds(row0, nr)],
                                     y_hbm.at[pl.ds(pl.multiple_of(base + row0, MOE_CH), nr)], wsem)

    def block(ch0, nr, first, last):
        row0 = pl.multiple_of(ch0 * MOE_CH, MOE_CH)
        rs = pl.ds(row0, nr)
        if first:
            for c in range(nr // MOE_CH):
                wait_gather(ch0 + c)
            xn_ref[rs, :] = _rms(acc_ref[rs, :], g_ref[...]).astype(BF)
        xn = xn_ref[rs, :]
        gate = _bdot(xn, wgb_ref[...])
        up = _bdot(xn, wub_ref[...])
        act = (gate * jax.nn.sigmoid(gate) * up).astype(BF)
        out = _bdot(act, wdb_ref[...])
        if first:
            acc_ref[rs, :] = out
        else:
            acc_ref[rs, :] += out
        if last:
            writeback(row0, nr).start()

    def sweep(first, last):
        if first:
            lax.fori_loop(0, nch, lambda ch, c: (issue_gather(ch), c)[1], 0)
        lax.fori_loop(0, nbig, lambda b, c: (block(big * b, big * MOE_CH, first, last), c)[1], 0)
        rems = remainders()
        for ch0, size, present in rems:
            @pl.when(present)
            def _():
                block(ch0, size * MOE_CH, first, last)

        if last:
            lax.fori_loop(0, nbig, lambda b, c: (writeback(0, big * MOE_CH).wait(), c)[1], 0)
            for ch0, size, present in rems:
                @pl.when(present)
                def _():
                    writeback(0, size * MOE_CH).wait()

    @pl.when(nch > 0)
    def _():
        wgb_ref[...] = wg_ref[0].astype(BF)
        wub_ref[...] = wu_ref[0].astype(BF)
        wdb_ref[...] = wd_ref[0].astype(BF)

        @pl.when(f == 0)
        def _():
            sweep(True, False)

        @pl.when(jnp.logical_and(f > 0, f < N_FF - 1))
        def _():
            sweep(False, False)

        @pl.when(f == N_FF - 1)
        def _():
            sweep(False, True)

    @pl.when(jnp.logical_and(t == MOE_NT - 1, f == N_FF - 1))
    def _():
        used = (base + nch * MOE_CH) // MOE_CH
        zero_ref[...] = jnp.zeros(zero_ref.shape, F32)

        def fill(c):
            return pltpu.make_async_copy(
                zero_ref, y_hbm.at[pl.ds(pl.multiple_of(c * MOE_CH, MOE_CH), MOE_CH)], wsem)

        lax.fori_loop(used, MOE_ROWS // MOE_CH, lambda c, k: (fill(c).start(), k)[1], 0)
        lax.fori_loop(used, MOE_ROWS // MOE_CH, lambda c, k: (fill(0).wait(), k)[1], 0)


def _moe(tile_expert, tile_rows, ybase, src, h, g, wg, wu, wd):
    ff = lambda t, f, nr: jnp.where(nr[t] > 0, f, N_FF - 1)
    grid_spec = pltpu.PrefetchScalarGridSpec(
        num_scalar_prefetch=4,
        grid=(MOE_NT, N_FF),
        in_specs=[
            pl.BlockSpec(memory_space=pl.ANY),
            pl.BlockSpec((1, D_MODEL), lambda t, f, te, nr, yb, sr: (0, 0)),
            pl.BlockSpec((1, D_MODEL, TF), lambda t, f, te, nr, yb, sr: (te[t], 0, ff(t, f, nr))),
            pl.BlockSpec((1, D_MODEL, TF), lambda t, f, te, nr, yb, sr: (te[t], 0, ff(t, f, nr))),
            pl.BlockSpec((1, TF, D_MODEL), lambda t, f, te, nr, yb, sr: (te[t], ff(t, f, nr), 0)),
        ],
        out_specs=pl.BlockSpec(memory_space=pl.ANY),
        scratch_shapes=[
            pltpu.VMEM((MOE_TM, D_MODEL), F32),
            pltpu.VMEM((MOE_TM, D_MODEL), BF),
            pltpu.VMEM((D_MODEL, TF), BF),
            pltpu.VMEM((D_MODEL, TF), BF),
            pltpu.VMEM((TF, D_MODEL), BF),
            pltpu.VMEM((MOE_CH, D_MODEL), F32),
            pltpu.SemaphoreType.DMA((N_CH,)),
            pltpu.SemaphoreType.DMA(()),
        ],
    )
    return pl.pallas_call(
        _moe_body,
        grid_spec=grid_spec,
        out_shape=jax.ShapeDtypeStruct((MOE_ROWS, D_MODEL), F32),
        compiler_params=_cparams(("arbitrary", "arbitrary")),
        name="moe",
    )(tile_expert, tile_rows, ybase, src, h, g, wg, wu, wd)


def _combine_body(pos_ref, h_ref, meta_ref, gf_ref, y_hbm, yp_ref, ys_ref, yb_ref, sems):
    i = pl.program_id(0)
    slot = i % 2

    def fetch(tile, sl):
        rows_per_iter = DMA_UNROLL // TOP_K

        def issue(r0, carry):
            for j in range(rows_per_iter):
                r = r0 * rows_per_iter + j
                for k in range(TOP_K):
                    pltpu.make_async_copy(y_hbm.at[pl.ds(pos_ref[TOP_K * (tile * TC + r) + k], 1)],
                                          yb_ref.at[sl, k, pl.ds(r, 1)], sems.at[sl]).start()
            return carry
        lax.fori_loop(0, TC // rows_per_iter, issue, 0)

    @pl.when(i == 0)
    def _():
        fetch(0, 0)

    @pl.when(i + 1 < N_TC)
    def _():
        fetch(i + 1, 1 - slot)

    for k in range(TOP_K):
        pltpu.make_async_copy(y_hbm.at[pl.ds(0, TC)], yb_ref.at[slot, k], sems.at[slot]).wait()
    meta = meta_ref[...]
    out = h_ref[...] + meta[:, 2:3] * yb_ref[slot, 0] + meta[:, 3:4] * yb_ref[slot, 1]
    out = _rms(out, gf_ref[...])

    @pl.when(i < N_TC_P)
    def _():
        yp_ref[...] = out

    @pl.when(i >= N_TC_P)
    def _():
        ys_ref[...] = out


def _combine(pos, h, meta, gf, y):
    grid_spec = pltpu.PrefetchScalarGridSpec(
        num_scalar_prefetch=1,
        grid=(N_TC,),
        in_specs=[
            pl.BlockSpec((TC, D_MODEL), lambda i, pos: (i, 0)),
            pl.BlockSpec((TC, 128), lambda i, pos: (i, 0)),
            pl.BlockSpec((1, D_MODEL), lambda i, pos: (0, 0)),
            pl.BlockSpec(memory_space=pl.ANY),
        ],
        out_specs=[
            pl.BlockSpec((TC, D_MODEL), lambda i, pos: (jnp.minimum(i, N_TC_P - 1), 0)),
            pl.BlockSpec((T_S, D_MODEL), lambda i, pos: (0, 0)),
        ],
        scratch_shapes=[
            pltpu.VMEM((2, TOP_K, TC, D_MODEL), F32),
            pltpu.SemaphoreType.DMA((2,)),
        ],
    )
    return pl.pallas_call(
        _combine_body,
        grid_spec=grid_spec,
        out_shape=[
            jax.ShapeDtypeStruct((T_P, D_MODEL), F32),
            jax.ShapeDtypeStruct((T_S, D_MODEL), F32),
        ],
        compiler_params=_cparams(("arbitrary",)),
        name="combine",
    )(pos, h, meta, gf, y)


def _routing_tables(experts):
    flat = experts.reshape(-1)
    onehot = (flat[:, None] == jnp.arange(N_EXPERTS, dtype=jnp.int32)[None, :]).astype(jnp.int32)
    csum = jnp.cumsum(onehot, axis=0)
    rank = jnp.take_along_axis(csum, flat[:, None], axis=1)[:, 0] - 1
    counts = csum[-1]
    tiles_per = (counts + MOE_TM - 1) // MOE_TM
    tile_end = jnp.cumsum(tiles_per)
    tile_start = tile_end - tiles_per
    n_used = tile_end[-1]
    tid = jnp.arange(MOE_NT, dtype=jnp.int32)
    last_tile = jnp.maximum(n_used - 1, 0)
    te = jnp.sum(tile_end[None, :] <= jnp.minimum(tid, last_tile)[:, None], axis=1).astype(jnp.int32)
    te = jnp.minimum(te, N_EXPERTS - 1)
    rows = jnp.clip(counts[te] - (tid - tile_start[te]) * MOE_TM, 0, MOE_TM)
    rows = jnp.where(tid < n_used, rows, 0).astype(jnp.int32)
    rows_pad = (rows + MOE_CH - 1) // MOE_CH * MOE_CH
    ybase = (jnp.cumsum(rows_pad) - rows_pad).astype(jnp.int32)
    pos = (ybase[tile_start[flat] + rank // MOE_TM] + rank % MOE_TM).astype(jnp.int32)
    tok = jnp.arange(T * TOP_K, dtype=jnp.int32) // TOP_K
    src = jnp.zeros((MOE_ROWS,), jnp.int32).at[pos].set(tok)
    return te, rows, ybase, src, pos


def _rope_tables():
    half = QK_ROPE // 2
    inv = 1.0 / (ROPE_THETA ** (jnp.arange(half, dtype=F32) * 2.0 / QK_ROPE))
    pos = jnp.concatenate([jnp.tile(jnp.arange(SEQ), BATCH), jnp.full((T_S,), PAST_LEN)])
    ang = pos.astype(F32)[:, None] * inv[None, :]
    cos, sin = jnp.cos(ang), jnp.sin(ang)
    zero = jnp.zeros((T, 128 - QK_ROPE), F32)
    return (jnp.concatenate([cos, cos, zero], axis=1),
            jnp.concatenate([-sin, sin, zero], axis=1))


def kernel(x_prompt, x_sample, state_pool, cache_mla, page_table, norm_mix, norm_ffn, w_in, g_v, w_s, b_s, w_pool, pool_scale, w_o_mix, w_ffn_gate, w_ffn_up, w_ffn_down, w_dqkv, g_q, g_kv, w_uq, w_uk, w_uv, w_o_attn, w_router, w_exp_gate, w_exp_up, w_exp_down, norm_final):
    x = jnp.concatenate([x_prompt.reshape(T_P, D_MODEL), x_sample.reshape(T_S, D_MODEL)], axis=0)

    y = _inproj(x, norm_mix[0:1], w_in[0], g_v)
    b_bcast = jnp.broadcast_to(b_s[0][:, :, None], (A_HEADS, CHUNK, CHUNK))
    a_out = _gate(y, w_s[0], b_bcast)
    hist2d = state_pool[0].reshape(T_S, POOL_HIST * B_WIDTH)
    b_out = _pool(y, hist2d, w_pool[0], pool_scale[0])
    h = _proj_res([a_out, b_out], w_o_mix[0], x, "mix_out")
    h = _ffn(h, norm_ffn[0:1], w_ffn_gate[0], w_ffn_up[0], w_ffn_down[0])

    pool_state_prompt = jnp.stack(
        [y[2, (b + 1) * SEQ - POOL_HIST:(b + 1) * SEQ] for b in range(BATCH)])[None]
    z_s = y[2, T_P:]
    pool_state_sample = jnp.concatenate([state_pool[0][:, 1:], z_s[:, None, :]], axis=1)[None]
    chunk_v_sample = y[1, T_P:].reshape(1, T_S, 1, A_WIDTH)

    cos_t, sin_t = _rope_tables()
    w_dqkv_pad = jnp.pad(w_dqkv[0], ((0, 0), (0, 128 - QK_ROPE)))
    cq, ckv, pe, lat = _dqkv(h, norm_mix[1:2], w_dqkv_pad, g_q, g_kv, cos_t, sin_t)
    wq_pad = jnp.pad(w_uq[0].reshape(Q_LORA, N_HEADS, QK_NOPE + QK_ROPE),
                     ((0, 0), (0, 0), (0, HEAD_PAD - QK_NOPE - QK_ROPE))).reshape(Q_LORA, N_HEADS * HEAD_PAD)
    wk2d = w_uk[0].reshape(KV_LORA, N_HEADS * QK_NOPE)
    wv2d = w_uv[0].reshape(KV_LORA, N_HEADS * V_HEAD)
    q, k, vt = _qkv(cq, ckv, pe, cos_t, sin_t, wq_pad, wk2d, wv2d.T)
    attn = _flash(q, k, vt)

    qlat = _qlat(q, wk2d)
    cache_t = jnp.swapaxes(cache_mla, 2, 3).reshape(-1, LATENT, PAGE_SIZE)
    olat = _decode(page_table.reshape(-1),
                   qlat.reshape(T_S, N_HEADS, KV_LORA),
                   q[T_P:].reshape(T_S, N_HEADS, HEAD_PAD),
                   lat[T_P:],
                   cache_t)
    attn_s = _ouv(olat.reshape(T_S, N_HEADS * KV_LORA), wv2d)
    attn = jnp.concatenate([attn, attn_s], axis=0)
    h = _proj_res([attn], w_o_attn[0], h, "attn_out")

    mla_rows_prompt = lat[:T_P].reshape(1, BATCH, SEQ, LATENT)
    mla_rows_sample = lat[T_P:].reshape(1, T_S, 1, LATENT)

    w_router_pad = jnp.pad(w_router[0], ((0, 0), (0, 128 - N_EXPERTS)))
    meta = _router(h, norm_ffn[1:2], w_router_pad)
    experts = meta[:, :TOP_K].astype(jnp.int32)
    te, rows, ybase, src, pos = _routing_tables(experts)
    y_sorted = _moe(te, rows, ybase, src, h, norm_ffn[1:2],
                    w_exp_gate[0], w_exp_up[0], w_exp_down[0])
    y_p, y_s = _combine(pos, h, meta, norm_final.reshape(1, D_MODEL), y_sorted)

    return (y_p.reshape(BATCH, SEQ, D_MODEL), y_s.reshape(T_S, 1, D_MODEL),
            pool_state_prompt, pool_state_sample, chunk_v_sample,
            mla_rows_prompt, mla_rows_sample)
```

```python
import functools

import numpy as np
import jax
import jax.numpy as jnp
from jax import lax
from jax.experimental import pallas as pl
from jax.experimental.pallas import tpu as pltpu

D_MODEL = 2048
BATCH = 4
SEQ = 2048
DEC_BATCH = 128
PAST_LEN = 8192
PAGE_SIZE = 128
N_PAGES = PAST_LEN // PAGE_SIZE
A_WIDTH = 1024
A_HEADS = 8
A_HEAD_DIM = 128
CHUNK = 128
B_WIDTH = 1024
POOL_WINDOWS = (2, 4, 8, 16)
B_GROUP_DIM = 256
POOL_HIST = 15
N_HEADS = 16
QK_NOPE = 128
QK_ROPE = 64
V_HEAD = 128
Q_LORA = 512
KV_LORA = 512
LATENT = KV_LORA + QK_ROPE
ROPE_THETA = 10000.0
ATTN_SCALE = (QK_NOPE + QK_ROPE) ** -0.5
D_FF = 7168
N_EXPERTS = 8
TOP_K = 2
EPS = 1e-6

T_P = BATCH * SEQ
T_S = DEC_BATCH
T = T_P + T_S

BF = jnp.bfloat16
F32 = jnp.float32

VMEM_LIMIT_BYTES = 56 * 1024 * 1024

TM = 640
N_TM = T // TM
TAIL_P = T_P - (N_TM - 1) * TM
TM_FFN = 1040
TF = 256
TF_FFN = 512
HEAD_PAD = 256
TQ = 512
N_QT = SEQ // TQ
PAGES_PER_STEP = 32
N_KV_STEPS = N_PAGES // PAGES_PER_STEP
KV_SLOTS = 3
MOE_CH = 128
MOE_BLOCKS = (4, 2, 1)
MOE_TM = 18 * MOE_CH
DMA_UNROLL = 8
MOE_NT = (T * TOP_K) // MOE_TM + N_EXPERTS
MOE_ROWS = -(-(T * TOP_K + MOE_NT * (MOE_CH - 1)) // MOE_CH) * MOE_CH
TC = 128
N_TC = T // TC
N_TC_P = T_P // TC


def _cparams(sem):
    return pltpu.CompilerParams(dimension_semantics=sem, vmem_limit_bytes=VMEM_LIMIT_BYTES)


def _rms(x, g):
    return x * lax.rsqrt(jnp.mean(x * x, axis=-1, keepdims=True) + EPS) * g


def _gelu(x):
    return 0.5 * x * (1.0 + lax.erf(x * np.float32(np.sqrt(0.5))))


def _bdot(a, b):
    return jnp.dot(a, b, preferred_element_type=F32)


def _bdot_nt(a, b):
    return lax.dot_general(a, b, (((1,), (1,)), ((), ())), preferred_element_type=F32)


def _rope128(x, c, s):
    swapped = pltpu.roll(x, 96, axis=1) + pltpu.roll(x, 32, axis=1)
    return x * c + swapped * s


def _inproj_body(xp_ref, xs_ref, g_ref, w_ref, gv_ref, o_ref, xn_ref):
    j = pl.program_id(0)
    xn_ref[...] = _rms(xp_ref[...], g_ref[...]).astype(BF)

    @pl.when(pl.program_id(1) == N_TM - 1)
    def _():
        xn_ref[TAIL_P:, :] = _rms(xs_ref[...], g_ref[...]).astype(BF)

    y = _bdot(xn_ref[...], w_ref[...].astype(BF))

    @pl.when(j == 0)
    def _():
        o_ref[0] = _gelu(y)

    @pl.when(j == 1)
    def _():
        o_ref[0] = _rms(_gelu(y), gv_ref[...])

    @pl.when(j == 2)
    def _():
        o_ref[0] = y


def _inproj(x_p, x_s, g, w, gv):
    return pl.pallas_call(
        _inproj_body,
        grid=(3, N_TM),
        in_specs=[
            pl.BlockSpec((TM, D_MODEL), lambda j, i: (i, 0)),
            pl.BlockSpec((T_S, D_MODEL), lambda j, i: (0, 0)),
            pl.BlockSpec((1, D_MODEL), lambda j, i: (0, 0)),
            pl.BlockSpec((D_MODEL, A_WIDTH), lambda j, i: (0, j)),
            pl.BlockSpec((1, A_WIDTH), lambda j, i: (0, 0)),
        ],
        out_specs=pl.BlockSpec((1, TM, A_WIDTH), lambda j, i: (j, i, 0)),
        out_shape=jax.ShapeDtypeStruct((3, T, A_WIDTH), F32),
        scratch_shapes=[pltpu.VMEM((TM, D_MODEL), BF)],
        compiler_params=_cparams(("arbitrary", "arbitrary")),
        name="inproj",
    )(x_p, x_s, g, w, gv)


def _gate_body(u_ref, v_ref, ws_ref, bb_ref, a_ref):
    is_sample = pl.program_id(0) >= N_TC_P
    row = lax.broadcasted_iota(jnp.int32, (CHUNK, CHUNK), 0)
    col = lax.broadcasted_iota(jnp.int32, (CHUNK, CHUNK), 1)
    for h in range(A_HEADS):
        w = ws_ref[h]
        w_chunk = jnp.where(col <= row, w, 0.0)
        w_single = jnp.where(col == row, jnp.broadcast_to(w[0:1, 0:1], (CHUNK, CHUNK)), 0.0)
        w_eff = jnp.where(is_sample, w_single, w_chunk).astype(BF)
        b = bb_ref[h]
        b_eff = jnp.where(is_sample, jnp.broadcast_to(b[0:1, :], (CHUNK, CHUNK)), b)
        sl = slice(h * A_HEAD_DIM, (h + 1) * A_HEAD_DIM)
        mixed = _bdot(w_eff, v_ref[0, :, sl].astype(BF)) + b_eff
        a_ref[:, sl] = (u_ref[0, :, sl] * mixed).astype(BF)


def _gate(y, w_s, b_bcast):
    return pl.pallas_call(
        _gate_body,
        grid=(N_TC,),
        in_specs=[
            pl.BlockSpec((1, TC, A_WIDTH), lambda i: (0, i, 0)),
            pl.BlockSpec((1, TC, A_WIDTH), lambda i: (1, i, 0)),
            pl.BlockSpec((A_HEADS, CHUNK, CHUNK), lambda i: (0, 0, 0)),
            pl.BlockSpec((A_HEADS, CHUNK, CHUNK), lambda i: (0, 0, 0)),
        ],
        out_specs=pl.BlockSpec((TC, A_WIDTH), lambda i: (i, 0)),
        out_shape=jax.ShapeDtypeStruct((T, A_WIDTH), BF),
        compiler_params=_cparams(("arbitrary",)),
        name="gate",
    )(y, y, w_s, b_bcast)


def _pool_body(z_ref, hist_ref, wp_ref, ps_ref, o_ref):
    b = pl.program_id(0)

    def project(pooled, g):
        out = _bdot(pooled.astype(BF), wp_ref[g].astype(BF)) * ps_ref[g:g + 1, :]
        return out.astype(BF)

    @pl.when(b < BATCH)
    def _():
        row = lax.broadcasted_iota(jnp.int32, (SEQ, B_GROUP_DIM), 0)
        for g, w in enumerate(POOL_WINDOWS):
            sl = slice(g * B_GROUP_DIM, (g + 1) * B_GROUP_DIM)
            z = z_ref[0, :, sl]
            s = z
            k = 1
            while k < w:
                s = s + jnp.where(row >= k, pltpu.roll(s, k, axis=0), 0.0)
                k *= 2
            cnt = jnp.minimum(row + 1, w).astype(F32)
            o_ref[:, sl] = project(s / cnt - z, g)

    @pl.when(b == BATCH)
    def _():
        for g, w in enumerate(POOL_WINDOWS):
            sl = slice(g * B_GROUP_DIM, (g + 1) * B_GROUP_DIM)
            z = z_ref[0, 0:T_S, sl]
            s = z
            for k in range(POOL_HIST + 1 - w, POOL_HIST):
                s = s + hist_ref[:, k * B_WIDTH + g * B_GROUP_DIM:k * B_WIDTH + (g + 1) * B_GROUP_DIM]
            o_ref[0:T_S, sl] = project(s / np.float32(w) - z, g)


def _pool(y, hist2d, w_pool, pool_scale):
    return pl.pallas_call(
        _pool_body,
        grid=(BATCH + 1,),
        in_specs=[
            pl.BlockSpec((1, SEQ, B_WIDTH), lambda b: (2, b, 0)),
            pl.BlockSpec((T_S, POOL_HIST * B_WIDTH), lambda b: (0, 0)),
            pl.BlockSpec((len(POOL_WINDOWS), B_GROUP_DIM, B_GROUP_DIM), lambda b: (0, 0, 0)),
            pl.BlockSpec((len(POOL_WINDOWS), B_GROUP_DIM), lambda b: (0, 0)),
        ],
        out_specs=pl.BlockSpec((SEQ, B_WIDTH), lambda b: (b, 0)),
        out_shape=jax.ShapeDtypeStruct((T, B_WIDTH), BF),
        compiler_params=_cparams(("arbitrary",)),
        name="pool",
    )(y, hist2d, w_pool, pool_scale)


def _proj_res_body(*refs, split):
    refs = list(refs)
    parts = []
    for is_pair in split:
        parts.append((refs.pop(0), refs.pop(0) if is_pair else None))
    w_ref, o_ref = refs
    acts, (res_ref, res_tail_ref) = parts[:-1], parts[-1]
    w = w_ref[...].astype(BF)
    acc = res_ref[...]
    off = 0
    for a_ref, _ in acts:
        k = a_ref.shape[1]
        acc = acc + _bdot(a_ref[...], w[off:off + k, :])
        off += k
    o_ref[...] = acc

    if any(split):
        @pl.when(pl.program_id(1) == N_TM - 1)
        def _():
            tail = res_ref[TAIL_P:, :] if res_tail_ref is None else res_tail_ref[...]
            off = 0
            for a_ref, a_tail_ref in acts:
                k = a_ref.shape[1]
                a = a_ref[TAIL_P:, :] if a_tail_ref is None else a_tail_ref[...]
                tail = tail + _bdot(a, w[off:off + k, :])
                off += k
            o_ref[TAIL_P:, :] = tail


def _proj_res(acts, w, res, name):
    tn = 1024
    k_total = w.shape[0]
    operands, in_specs, split = [], [], []
    for n, x in enumerate(list(acts) + [res]):
        is_res = n == len(acts)
        pair = isinstance(x, tuple)
        split.append(pair)
        for part, rows in zip(x if pair else (x,), (TM, T_S)):
            cols = tn if is_res else part.shape[1]
            operands.append(part)
            if rows == TM:
                in_specs.append(pl.BlockSpec((TM, cols), (lambda j, i: (i, j)) if is_res
                                             else (lambda j, i: (i, 0))))
            else:
                in_specs.append(pl.BlockSpec((T_S, cols), (lambda j, i: (0, j)) if is_res
                                             else (lambda j, i: (0, 0))))
    return pl.pallas_call(
        functools.partial(_proj_res_body, split=tuple(split)),
        grid=(D_MODEL // tn, N_TM),
        in_specs=in_specs + [pl.BlockSpec((k_total, tn), lambda j, i: (0, j))],
        out_specs=pl.BlockSpec((TM, tn), lambda j, i: (i, j)),
        out_shape=jax.ShapeDtypeStruct((T, D_MODEL), F32),
        compiler_params=_cparams(("arbitrary", "arbitrary")),
        name=name,
    )(*operands, w)


def _ffn_body(x_hbm, g_ref, wg_ref, wu_ref, wd_ref, o_hbm, acc_ref, xn_ref, sem):
    i = pl.program_id(0)
    f = pl.program_id(1)
    rows = pl.ds(i * TM_FFN, TM_FFN)

    @pl.when(f == 0)
    def _():
        cp = pltpu.make_async_copy(x_hbm.at[rows], acc_ref, sem)
        cp.start()
        cp.wait()
        xn_ref[...] = _rms(acc_ref[...], g_ref[...]).astype(BF)

    xn = xn_ref[...]
    gate = _bdot(xn, wg_ref[...].astype(BF))
    up = _bdot(xn, wu_ref[...].astype(BF))
    act = (gate * jax.nn.sigmoid(gate) * up).astype(BF)
    acc_ref[...] += _bdot(act, wd_ref[...].astype(BF))

    @pl.when(f == pl.num_programs(1) - 1)
    def _():
        cp = pltpu.make_async_copy(acc_ref, o_hbm.at[rows], sem)
        cp.start()
        cp.wait()


def _ffn(x, g, wg, wu, wd):
    return pl.pallas_call(
        _ffn_body,
        grid=(T // TM_FFN, D_FF // TF_FFN),
        in_specs=[
            pl.BlockSpec(memory_space=pl.ANY),
            pl.BlockSpec((1, D_MODEL), lambda i, f: (0, 0)),
            pl.BlockSpec((D_MODEL, TF_FFN), lambda i, f: (0, f)),
            pl.BlockSpec((D_MODEL, TF_FFN), lambda i, f: (0, f)),
            pl.BlockSpec((TF_FFN, D_MODEL), lambda i, f: (f, 0)),
        ],
        out_specs=pl.BlockSpec(memory_space=pl.ANY),
        out_shape=jax.ShapeDtypeStruct((T, D_MODEL), F32),
        scratch_shapes=[pltpu.VMEM((TM_FFN, D_MODEL), F32),
                        pltpu.VMEM((TM_FFN, D_MODEL), BF),
                        pltpu.SemaphoreType.DMA(())],
        compiler_params=_cparams(("arbitrary", "arbitrary")),
        name="ffn",
    )(x, g, wg, wu, wd)


def _dqkv_body(x_ref, g_ref, w_ref, gq_ref, gkv_ref, cos_ref, sin_ref,
               cq_ref, ckv_ref, pe_ref, lat_ref, lat_s_ref):
    xn = _rms(x_ref[...], g_ref[...]).astype(BF)
    d = _bdot(xn, w_ref[...].astype(BF))
    cq_ref[...] = _rms(d[:, :Q_LORA], gq_ref[...]).astype(BF)
    ckv = _rms(d[:, Q_LORA:Q_LORA + KV_LORA], gkv_ref[...])
    pe = _rope128(d[:, Q_LORA + KV_LORA:], cos_ref[...], sin_ref[...])
    ckv_ref[...] = ckv.astype(BF)
    pe_ref[...] = pe.astype(BF)
    lat_ref[:, :KV_LORA] = ckv
    lat_ref[:, KV_LORA:] = pe[:, :QK_ROPE]

    @pl.when(pl.program_id(0) == N_TM - 1)
    def _():
        lat_s_ref[:, :KV_LORA] = ckv[TAIL_P:, :]
        lat_s_ref[:, KV_LORA:] = pe[TAIL_P:, :QK_ROPE]


def _dqkv(x, g, w_pad, gq, gkv, cos_t, sin_t):
    n = w_pad.shape[1]
    row = lambda i: (i, 0)
    fix = lambda i: (0, 0)
    return pl.pallas_call(
        _dqkv_body,
        grid=(N_TM,),
        in_specs=[
            pl.BlockSpec((TM, D_MODEL), row),
            pl.BlockSpec((1, D_MODEL), fix),
            pl.BlockSpec((D_MODEL, n), fix),
            pl.BlockSpec((1, Q_LORA), fix),
            pl.BlockSpec((1, KV_LORA), fix),
            pl.BlockSpec((TM, 128), row),
            pl.BlockSpec((TM, 128), row),
        ],
        out_specs=[
            pl.BlockSpec((TM, Q_LORA), row),
            pl.BlockSpec((TM, KV_LORA), row),
            pl.BlockSpec((TM, 128), row),
            pl.BlockSpec((TM, LATENT), row),
            pl.BlockSpec((T_S, LATENT), fix),
        ],
        out_shape=[
            jax.ShapeDtypeStruct((T, Q_LORA), BF),
            jax.ShapeDtypeStruct((T, KV_LORA), BF),
            jax.ShapeDtypeStruct((T, 128), BF),
            jax.ShapeDtypeStruct((T_P, LATENT), F32),
            jax.ShapeDtypeStruct((T_S, LATENT), F32),
        ],
        compiler_params=_cparams(("arbitrary",)),
        name="dqkv",
    )(x, g, w_pad, gq, gkv, cos_t, sin_t)


HG = 4


def _qkv_body(cq_ref, ckv_ref, pe_ref, cos_ref, sin_ref, wq_ref, wk_ref, wvt_ref,
              q_ref, k_ref, vt_ref):
    q = _bdot(cq_ref[...], wq_ref[...].astype(BF)) * np.float32(ATTN_SCALE)
    ckv = ckv_ref[...]
    kn = _bdot(ckv, wk_ref[...].astype(BF))
    pe = pe_ref[...]
    c = cos_ref[...]
    s = sin_ref[...]
    for h in range(HG):
        o = h * HEAD_PAD
        q_ref[:, o:o + QK_NOPE] = q[:, o:o + QK_NOPE].astype(BF)
        q_ref[:, o + QK_NOPE:o + HEAD_PAD] = _rope128(q[:, o + QK_NOPE:o + HEAD_PAD], c, s).astype(BF)
        k_ref[:, o:o + QK_NOPE] = kn[:, h * QK_NOPE:(h + 1) * QK_NOPE].astype(BF)
        k_ref[:, o + QK_NOPE:o + HEAD_PAD] = pe
    vt_ref[...] = _bdot_nt(wvt_ref[...].astype(BF), ckv).astype(BF)


def _qkv(cq, ckv, pe, cos_t, sin_t, wq_pad, wk, wv_t):
    row = lambda j, i: (i, 0)
    col = lambda j, i: (0, j)
    out = lambda j, i: (i, j)
    return pl.pallas_call(
        _qkv_body,
        grid=(N_HEADS // HG, N_TM),
        in_specs=[
            pl.BlockSpec((TM, Q_LORA), row),
            pl.BlockSpec((TM, KV_LORA), row),
            pl.BlockSpec((TM, 128), row),
            pl.BlockSpec((TM, 128), row),
            pl.BlockSpec((TM, 128), row),
            pl.BlockSpec((Q_LORA, HG * HEAD_PAD), col),
            pl.BlockSpec((KV_LORA, HG * QK_NOPE), col),
            pl.BlockSpec((HG * V_HEAD, KV_LORA), lambda j, i: (j, 0)),
        ],
        out_specs=[
            pl.BlockSpec((TM, HG * HEAD_PAD), out),
            pl.BlockSpec((TM, HG * HEAD_PAD), out),
            pl.BlockSpec((HG * V_HEAD, TM), lambda j, i: (j, i)),
        ],
        out_shape=[
            jax.ShapeDtypeStruct((T, N_HEADS * HEAD_PAD), BF),
            jax.ShapeDtypeStruct((T, N_HEADS * HEAD_PAD), BF),
            jax.ShapeDtypeStruct((N_HEADS * V_HEAD, T), BF),
        ],
        compiler_params=_cparams(("arbitrary", "arbitrary")),
        name="qkv",
    )(cq, ckv, pe, cos_t, sin_t, wq_pad, wk, wv_t)


_PAIRS = [(qi, ki) for qi in range(N_QT) for ki in range(qi + 1)]


def _flash_body(qi_tab, ki_tab, q_ref, k_ref, vt_ref, o_ref, m_ref, l_ref, acc_ref):
    p = pl.program_id(1)
    qi = qi_tab[p]
    ki = ki_tab[p]

    @pl.when(ki == 0)
    def _():
        m_ref[...] = jnp.full(m_ref.shape, -jnp.inf, F32)
        l_ref[...] = jnp.zeros(l_ref.shape, F32)
        acc_ref[...] = jnp.zeros(acc_ref.shape, F32)

    def all_heads(on_diagonal):
        if on_diagonal:
            key = lax.broadcasted_iota(jnp.int32, (TQ, TQ), 0)
            qry = lax.broadcasted_iota(jnp.int32, (TQ, TQ), 1)
            visible = key <= qry
        for h in range(N_HEADS):
            qk = slice(h * HEAD_PAD, (h + 1) * HEAD_PAD)
            vs = slice(h * V_HEAD, (h + 1) * V_HEAD)
            st = _bdot_nt(k_ref[:, qk], q_ref[:, qk])
            if on_diagonal:
                st = jnp.where(visible, st, -jnp.inf)
            m_prev = m_ref[h]
            m_new = jnp.maximum(m_prev, jnp.max(st, axis=0, keepdims=True))
            alpha = jnp.exp(m_prev - m_new)
            pt = jnp.exp(st - m_new)
            l_ref[h] = alpha * l_ref[h] + jnp.sum(pt, axis=0, keepdims=True)
            acc_ref[vs, :] = alpha * acc_ref[vs, :] + _bdot(vt_ref[vs, :], pt.astype(BF))
            m_ref[h] = m_new

    @pl.when(ki < qi)
    def _():
        all_heads(False)

    @pl.when(ki == qi)
    def _():
        all_heads(True)
        for h in range(N_HEADS):
            vs = slice(h * V_HEAD, (h + 1) * V_HEAD)
            o_ref[:, vs] = (acc_ref[vs, :] / l_ref[h]).T.astype(BF)


def _flash(q, k, vt):
    qi_tab = jnp.asarray(np.array([p[0] for p in _PAIRS], np.int32))
    ki_tab = jnp.asarray(np.array([p[1] for p in _PAIRS], np.int32))
    grid_spec = pltpu.PrefetchScalarGridSpec(
        num_scalar_prefetch=2,
        grid=(BATCH, len(_PAIRS)),
        in_specs=[
            pl.BlockSpec((TQ, N_HEADS * HEAD_PAD), lambda b, p, qt, kt: (b * N_QT + qt[p], 0)),
            pl.BlockSpec((TQ, N_HEADS * HEAD_PAD), lambda b, p, qt, kt: (b * N_QT + kt[p], 0)),
            pl.BlockSpec((N_HEADS * V_HEAD, TQ), lambda b, p, qt, kt: (0, b * N_QT + kt[p])),
        ],
        out_specs=pl.BlockSpec((TQ, N_HEADS * V_HEAD), lambda b, p, qt, kt: (b * N_QT + qt[p], 0)),
        scratch_shapes=[
            pltpu.VMEM((N_HEADS, 1, TQ), F32),
            pltpu.VMEM((N_HEADS, 1, TQ), F32),
            pltpu.VMEM((N_HEADS * V_HEAD, TQ), F32),
        ],
    )
    return pl.pallas_call(
        _flash_body,
        grid_spec=grid_spec,
        out_shape=jax.ShapeDtypeStruct((T_P, N_HEADS * V_HEAD), BF),
        compiler_params=_cparams(("arbitrary", "arbitrary")),
        name="flash",
    )(qi_tab, ki_tab, q, k, vt)


def _qlat_body(q_ref, wk_ref, o_ref):
    o_ref[...] = _bdot_nt(q_ref[:, :QK_NOPE], wk_ref[...].astype(BF)).astype(BF)


def _qlat(q, wk):
    return pl.pallas_call(
        _qlat_body,
        grid=(N_HEADS,),
        in_specs=[
            pl.BlockSpec((T_S, HEAD_PAD), lambda h: (T_P // T_S, h)),
            pl.BlockSpec((KV_LORA, QK_NOPE), lambda h: (0, h)),
        ],
        out_specs=pl.BlockSpec((T_S, KV_LORA), lambda h: (0, h)),
        out_shape=jax.ShapeDtypeStruct((T_S, N_HEADS * KV_LORA), BF),
        compiler_params=_cparams(("arbitrary",)),
        name="qlat",
    )(q, wk)


def _decode_body(pt_ref, qlat_ref, q_ref, lat_ref, cache_hbm, o_ref,
                 buf, kb_ref, sems, m_ref, l_ref, acc_ref):
    b = pl.program_id(0)
    c = pl.program_id(1)
    step = b * N_KV_STEPS + c
    n_steps = DEC_BATCH * N_KV_STEPS
    slot = step % KV_SLOTS

    def page_copy(st, j, sl):
        pg = pt_ref[st * PAGES_PER_STEP + j]
        return pltpu.make_async_copy(cache_hbm.at[pg], buf.at[sl, j], sems.at[sl])

    def start_fetch(st, sl):
        for j in range(PAGES_PER_STEP):
            page_copy(st, j, sl).start()

    @pl.when(step == 0)
    def _():
        for s0 in range(KV_SLOTS - 1):
            start_fetch(s0, s0)

    ahead = step + (KV_SLOTS - 1)

    @pl.when(ahead < n_steps)
    def _():
        start_fetch(ahead, ahead % KV_SLOTS)

    qlat = qlat_ref[0]
    qpe = q_ref[0, :, QK_NOPE:QK_NOPE + QK_ROPE]

    @pl.when(c == 0)
    def _():
        lat = lat_ref[pl.ds(b, 1), :]
        s_self = (jnp.sum(qlat.astype(F32) * lat[:, :KV_LORA], axis=-1, keepdims=True)
                  + jnp.sum(qpe.astype(F32) * lat[:, KV_LORA:], axis=-1, keepdims=True))
        m_ref[...] = s_self
        l_ref[...] = jnp.ones(l_ref.shape, F32)
        acc_ref[...] = jnp.broadcast_to(lat[:, :KV_LORA], acc_ref.shape)

    for j in range(PAGES_PER_STEP):
        page_copy(step, j, slot).wait()

    s_parts = []
    for j in range(PAGES_PER_STEP):
        kb_ref[j] = buf[slot, j].astype(BF)
        s_parts.append(_bdot(qlat, kb_ref[j, :KV_LORA, :]) + _bdot(qpe, kb_ref[j, KV_LORA:, :]))
    s = jnp.concatenate(s_parts, axis=1)
    m_prev = m_ref[...]
    m_new = jnp.maximum(m_prev, jnp.max(s, axis=-1, keepdims=True))
    alpha = jnp.exp(m_prev - m_new)
    pexp = jnp.exp(s - m_new)
    l_ref[...] = alpha * l_ref[...] + jnp.sum(pexp, axis=-1, keepdims=True)
    pb = pexp.astype(BF)
    pv = jnp.zeros(acc_ref.shape, F32)
    for j in range(PAGES_PER_STEP):
        pv = pv + _bdot_nt(pb[:, j * PAGE_SIZE:(j + 1) * PAGE_SIZE], kb_ref[j, :KV_LORA, :])
    acc_ref[...] = alpha * acc_ref[...] + pv
    m_ref[...] = m_new

    @pl.when(c == N_KV_STEPS - 1)
    def _():
        o_ref[0] = acc_ref[...] / l_ref[...]


def _decode(page_table_flat, qlat3, q3, lat_s, cache_t):
    grid_spec = pltpu.PrefetchScalarGridSpec(
        num_scalar_prefetch=1,
        grid=(DEC_BATCH, N_KV_STEPS),
        in_specs=[
            pl.BlockSpec((1, N_HEADS, KV_LORA), lambda b, c, pt: (b, 0, 0)),
            pl.BlockSpec((1, N_HEADS, HEAD_PAD), lambda b, c, pt: (b, 0, 0)),
            pl.BlockSpec((T_S, LATENT), lambda b, c, pt: (0, 0)),
            pl.BlockSpec(memory_space=pl.ANY),
        ],
        out_specs=pl.BlockSpec((1, N_HEADS, KV_LORA), lambda b, c, pt: (b, 0, 0)),
        scratch_shapes=[
            pltpu.VMEM((KV_SLOTS, PAGES_PER_STEP, LATENT, PAGE_SIZE), F32),
            pltpu.VMEM((PAGES_PER_STEP, LATENT, PAGE_SIZE), BF),
            pltpu.SemaphoreType.DMA((KV_SLOTS,)),
            pltpu.VMEM((N_HEADS, 1), F32),
            pltpu.VMEM((N_HEADS, 1), F32),
            pltpu.VMEM((N_HEADS, KV_LORA), F32),
        ],
    )
    return pl.pallas_call(
        _decode_body,
        grid_spec=grid_spec,
        out_shape=jax.ShapeDtypeStruct((DEC_BATCH, N_HEADS, KV_LORA), F32),
        compiler_params=_cparams(("arbitrary", "arbitrary")),
        name="decode",
    )(page_table_flat, qlat3, q3, lat_s, cache_t)


def _ouv_body(o_ref, wv_ref, out_ref):
    out_ref[...] = _bdot(o_ref[...].astype(BF), wv_ref[...].astype(BF)).astype(BF)


def _ouv(olat2d, wv):
    return pl.pallas_call(
        _ouv_body,
        grid=(N_HEADS,),
        in_specs=[
            pl.BlockSpec((T_S, KV_LORA), lambda h: (0, h)),
            pl.BlockSpec((KV_LORA, V_HEAD), lambda h: (0, h)),
        ],
        out_specs=pl.BlockSpec((T_S, V_HEAD), lambda h: (0, h)),
        out_shape=jax.ShapeDtypeStruct((T_S, N_HEADS * V_HEAD), BF),
        compiler_params=_cparams(("arbitrary",)),
        name="ouv",
    )(olat2d, wv)


def _router_body(x_ref, g_ref, w_ref, meta_ref):
    xn = _rms(x_ref[...], g_ref[...])
    w = w_ref[...]
    xh = xn.astype(BF)
    xl = (xn - xh.astype(F32)).astype(BF)
    wh = w.astype(BF)
    wl = (w - wh.astype(F32)).astype(BF)
    logits = _bdot(xh, wh) + (_bdot(xh, wl) + _bdot(xl, wh))
    lane = lax.broadcasted_iota(jnp.int32, logits.shape, 1).astype(F32)
    lg = jnp.where(lane < N_EXPERTS, logits, -jnp.inf)
    t1 = jnp.max(lg, axis=-1, keepdims=True)
    i1 = jnp.min(jnp.where(lg == t1, lane, 128.0), axis=-1, keepdims=True)
    lg2 = jnp.where(lane == i1, -jnp.inf, lg)
    t2 = jnp.max(lg2, axis=-1, keepdims=True)
    i2 = jnp.min(jnp.where(lg2 == t2, lane, 128.0), axis=-1, keepdims=True)
    e = jnp.exp(t2 - t1)
    g1 = 1.0 / (1.0 + e)
    g2 = e / (1.0 + e)
    meta = jnp.where(lane == 0, i1, jnp.where(lane == 1, i2, jnp.where(lane == 2, g1, g2)))
    meta_ref[...] = meta


def _router(x, g, w_pad):
    return pl.pallas_call(
        _router_body,
        grid=(N_TM,),
        in_specs=[
            pl.BlockSpec((TM, D_MODEL), lambda i: (i, 0)),
            pl.BlockSpec((1, D_MODEL), lambda i: (0, 0)),
            pl.BlockSpec((D_MODEL, 128), lambda i: (0, 0)),
        ],
        out_specs=pl.BlockSpec((TM, 128), lambda i: (i, 0)),
        out_shape=jax.ShapeDtypeStruct((T, 128), F32),
        compiler_params=_cparams(("arbitrary",)),
        name="router",
    )(x, g, w_pad)


N_FF = D_FF // TF
N_CH = MOE_TM // MOE_CH


def _moe_body(te_ref, rows_ref, ybase_ref, src_ref, h_hbm, g_ref, wg_ref, wu_ref, wd_ref, y_hbm,
              acc_ref, xn_ref, wgb_ref, wub_ref, wdb_ref, zero_ref, gsems, wsem):
    t = pl.program_id(0)
    f = pl.program_id(1)
    nch = (rows_ref[t] + MOE_CH - 1) // MOE_CH
    big = MOE_BLOCKS[0]
    nbig = nch // big
    base = pl.multiple_of(ybase_ref[t], MOE_CH)

    def remainders():
        out = []
        ch0 = nbig * big
        left = nch - ch0
        for size in MOE_BLOCKS[1:]:
            present = left >= size
            out.append((ch0, size, present))
            ch0 = ch0 + jnp.where(present, size, 0)
            left = left - jnp.where(present, size, 0)
        return out

    def issue_gather(ch):
        def body(r0, carry):
            for j in range(DMA_UNROLL):
                r = ch * MOE_CH + r0 * DMA_UNROLL + j
                pltpu.make_async_copy(h_hbm.at[pl.ds(src_ref[base + r], 1)],
                                      acc_ref.at[pl.ds(r, 1)], gsems.at[ch]).start()
            return carry
        lax.fori_loop(0, MOE_CH // DMA_UNROLL, body, 0)

    def wait_gather(ch):
        pltpu.make_async_copy(h_hbm.at[pl.ds(0, MOE_CH)],
                              acc_ref.at[pl.ds(ch * MOE_CH, MOE_CH)], gsems.at[ch]).wait()

    def writeback(row0, nr):
        return pltpu.make_async_copy(acc_ref.at[pl.ds(row0, nr)],
                                     y_hbm.at[pl.ds(pl.multiple_of(base + row0, MOE_CH), nr)], wsem)

    def cast_weights():
        wg = wg_ref[0].astype(BF)
        wu = wu_ref[0].astype(BF)
        wd = wd_ref[0].astype(BF)
        wgb_ref[...] = wg
        wub_ref[...] = wu
        wdb_ref[...] = wd
        return wg, wu, wd

    def block(ch0, nr, first, last, cast=False):
        row0 = pl.multiple_of(ch0 * MOE_CH, MOE_CH)
        rs = pl.ds(row0, nr)
        if first:
            for c in range(nr // MOE_CH):
                wait_gather(ch0 + c)
            xn_ref[rs, :] = _rms(acc_ref[rs, :], g_ref[...]).astype(BF)
        xn = xn_ref[rs, :]
        wg, wu, wd = cast_weights() if cast else (wgb_ref[...], wub_ref[...], wdb_ref[...])
        gate = _bdot(xn, wg)
        up = _bdot(xn, wu)
        act = (gate * jax.nn.sigmoid(gate) * up).astype(BF)
        out = _bdot(act, wd)
        if first:
            acc_ref[rs, :] = out
        else:
            acc_ref[rs, :] += out
        if last:
            writeback(row0, nr).start()

    def sweep(first, last):
        if first:
            lax.fori_loop(0, nch, lambda ch, c: (issue_gather(ch), c)[1], 0)
        @pl.when(nbig > 0)
        def _():
            block(0, big * MOE_CH, first, last, cast=True)

        @pl.when(nbig == 0)
        def _():
            cast_weights()

        lax.fori_loop(1, nbig, lambda b, c: (block(big * b, big * MOE_CH, first, last), c)[1], 0)
        rems = remainders()
        for ch0, size, present in rems:
            @pl.when(present)
            def _():
                block(ch0, size * MOE_CH, first, last)

        if last:
            lax.fori_loop(0, nbig, lambda b, c: (writeback(0, big * MOE_CH).wait(), c)[1], 0)
            for ch0, size, present in rems:
                @pl.when(present)
                def _():
                    writeback(0, size * MOE_CH).wait()

    @pl.when(nch > 0)
    def _():
        @pl.when(f == 0)
        def _():
            sweep(True, False)

        @pl.when(jnp.logical_and(f > 0, f < N_FF - 1))
        def _():
            sweep(False, False)

        @pl.when(f == N_FF - 1)
        def _():
            sweep(False, True)

    @pl.when(jnp.logical_and(t == MOE_NT - 1, f == N_FF - 1))
    def _():
        used = (base + nch * MOE_CH) // MOE_CH
        zero_ref[...] = jnp.zeros(zero_ref.shape, F32)

        def fill(c):
            return pltpu.make_async_copy(
                zero_ref, y_hbm.at[pl.ds(pl.multiple_of(c * MOE_CH, MOE_CH), MOE_CH)], wsem)

        lax.fori_loop(used, MOE_ROWS // MOE_CH, lambda c, k: (fill(c).start(), k)[1], 0)
        lax.fori_loop(used, MOE_ROWS // MOE_CH, lambda c, k: (fill(0).wait(), k)[1], 0)


def _moe(tile_expert, tile_rows, ybase, src, h, g, wg, wu, wd):
    ff = lambda t, f, nr: jnp.where(nr[t] > 0, f, N_FF - 1)
    grid_spec = pltpu.PrefetchScalarGridSpec(
        num_scalar_prefetch=4,
        grid=(MOE_NT, N_FF),
        in_specs=[
            pl.BlockSpec(memory_space=pl.ANY),
            pl.BlockSpec((1, D_MODEL), lambda t, f, te, nr, yb, sr: (0, 0)),
            pl.BlockSpec((1, D_MODEL, TF), lambda t, f, te, nr, yb, sr: (te[t], 0, ff(t, f, nr))),
            pl.BlockSpec((1, D_MODEL, TF), lambda t, f, te, nr, yb, sr: (te[t], 0, ff(t, f, nr))),
            pl.BlockSpec((1, TF, D_MODEL), lambda t, f, te, nr, yb, sr: (te[t], ff(t, f, nr), 0)),
        ],
        out_specs=pl.BlockSpec(memory_space=pl.ANY),
        scratch_shapes=[
            pltpu.VMEM((MOE_TM, D_MODEL), F32),
            pltpu.VMEM((MOE_TM, D_MODEL), BF),
            pltpu.VMEM((D_MODEL, TF), BF),
            pltpu.VMEM((D_MODEL, TF), BF),
            pltpu.VMEM((TF, D_MODEL), BF),
            pltpu.VMEM((MOE_CH, D_MODEL), F32),
            pltpu.SemaphoreType.DMA((N_CH,)),
            pltpu.SemaphoreType.DMA(()),
        ],
    )
    return pl.pallas_call(
        _moe_body,
        grid_spec=grid_spec,
        out_shape=jax.ShapeDtypeStruct((MOE_ROWS, D_MODEL), F32),
        compiler_params=_cparams(("arbitrary", "arbitrary")),
        name="moe",
    )(tile_expert, tile_rows, ybase, src, h, g, wg, wu, wd)


def _combine_body(pos_ref, h_ref, meta_ref, gf_ref, y_hbm, yp_ref, ys_ref, yb_ref, sems):
    i = pl.program_id(0)
    slot = i % 2

    def fetch(tile, sl):
        rows_per_iter = DMA_UNROLL // TOP_K

        def issue(r0, carry):
            for j in range(rows_per_iter):
                r = r0 * rows_per_iter + j
                for k in range(TOP_K):
                    pltpu.make_async_copy(y_hbm.at[pl.ds(pos_ref[TOP_K * (tile * TC + r) + k], 1)],
                                          yb_ref.at[sl, k, pl.ds(r, 1)], sems.at[sl]).start()
            return carry
        lax.fori_loop(0, TC // rows_per_iter, issue, 0)

    @pl.when(i == 0)
    def _():
        fetch(0, 0)

    @pl.when(i + 1 < N_TC)
    def _():
        fetch(i + 1, 1 - slot)

    for k in range(TOP_K):
        pltpu.make_async_copy(y_hbm.at[pl.ds(0, TC)], yb_ref.at[slot, k], sems.at[slot]).wait()
    meta = meta_ref[...]
    out = h_ref[...] + meta[:, 2:3] * yb_ref[slot, 0] + meta[:, 3:4] * yb_ref[slot, 1]
    out = _rms(out, gf_ref[...])

    @pl.when(i < N_TC_P)
    def _():
        yp_ref[...] = out

    @pl.when(i >= N_TC_P)
    def _():
        ys_ref[...] = out


def _combine(pos, h, meta, gf, y):
    grid_spec = pltpu.PrefetchScalarGridSpec(
        num_scalar_prefetch=1,
        grid=(N_TC,),
        in_specs=[
            pl.BlockSpec((TC, D_MODEL), lambda i, pos: (i, 0)),
            pl.BlockSpec((TC, 128), lambda i, pos: (i, 0)),
            pl.BlockSpec((1, D_MODEL), lambda i, pos: (0, 0)),
            pl.BlockSpec(memory_space=pl.ANY),
        ],
        out_specs=[
            pl.BlockSpec((TC, D_MODEL), lambda i, pos: (jnp.minimum(i, N_TC_P - 1), 0)),
            pl.BlockSpec((T_S, D_MODEL), lambda i, pos: (0, 0)),
        ],
        scratch_shapes=[
            pltpu.VMEM((2, TOP_K, TC, D_MODEL), F32),
            pltpu.SemaphoreType.DMA((2,)),
        ],
    )
    return pl.pallas_call(
        _combine_body,
        grid_spec=grid_spec,
        out_shape=[
            jax.ShapeDtypeStruct((T_P, D_MODEL), F32),
            jax.ShapeDtypeStruct((T_S, D_MODEL), F32),
        ],
        compiler_params=_cparams(("arbitrary",)),
        name="combine",
    )(pos, h, meta, gf, y)


def _routing_tables(experts):
    flat = experts.reshape(-1)
    onehot = (flat[:, None] == jnp.arange(N_EXPERTS, dtype=jnp.int32)[None, :]).astype(jnp.int32)
    csum = jnp.cumsum(onehot, axis=0)
    rank = jnp.take_along_axis(csum, flat[:, None], axis=1)[:, 0] - 1
    counts = csum[-1]
    tiles_per = (counts + MOE_TM - 1) // MOE_TM
    tile_end = jnp.cumsum(tiles_per)
    tile_start = tile_end - tiles_per
    n_used = tile_end[-1]
    tid = jnp.arange(MOE_NT, dtype=jnp.int32)
    last_tile = jnp.maximum(n_used - 1, 0)
    te = jnp.sum(tile_end[None, :] <= jnp.minimum(tid, last_tile)[:, None], axis=1).astype(jnp.int32)
    te = jnp.minimum(te, N_EXPERTS - 1)
    rows = jnp.clip(counts[te] - (tid - tile_start[te]) * MOE_TM, 0, MOE_TM)
    rows = jnp.where(tid < n_used, rows, 0).astype(jnp.int32)
    rows_pad = (rows + MOE_CH - 1) // MOE_CH * MOE_CH
    ybase = (jnp.cumsum(rows_pad) - rows_pad).astype(jnp.int32)
    pos = (ybase[tile_start[flat] + rank // MOE_TM] + rank % MOE_TM).astype(jnp.int32)
    tok = jnp.arange(T * TOP_K, dtype=jnp.int32) // TOP_K
    src = jnp.zeros((MOE_ROWS,), jnp.int32).at[pos].set(tok)
    return te, rows, ybase, src, pos


def _rope_tables():
    half = QK_ROPE // 2
    inv = 1.0 / (ROPE_THETA ** (jnp.arange(half, dtype=F32) * 2.0 / QK_ROPE))
    pos = jnp.concatenate([jnp.tile(jnp.arange(SEQ), BATCH), jnp.full((T_S,), PAST_LEN)])
    ang = pos.astype(F32)[:, None] * inv[None, :]
    cos, sin = jnp.cos(ang), jnp.sin(ang)
    zero = jnp.zeros((T, 128 - QK_ROPE), F32)
    return (jnp.concatenate([cos, cos, zero], axis=1),
            jnp.concatenate([-sin, sin, zero], axis=1))


def kernel(x_prompt, x_sample, state_pool, cache_mla, page_table, norm_mix, norm_ffn, w_in, g_v, w_s, b_s, w_pool, pool_scale, w_o_mix, w_ffn_gate, w_ffn_up, w_ffn_down, w_dqkv, g_q, g_kv, w_uq, w_uk, w_uv, w_o_attn, w_router, w_exp_gate, w_exp_up, w_exp_down, norm_final):
    x_p = x_prompt.reshape(T_P, D_MODEL)
    x_s = x_sample.reshape(T_S, D_MODEL)

    y = _inproj(x_p, x_s, norm_mix[0:1], w_in[0], g_v)
    b_bcast = jnp.broadcast_to(b_s[0][:, :, None], (A_HEADS, CHUNK, CHUNK))
    a_out = _gate(y, w_s[0], b_bcast)
    hist2d = state_pool[0].reshape(T_S, POOL_HIST * B_WIDTH)
    b_out = _pool(y, hist2d, w_pool[0], pool_scale[0])
    h = _proj_res([a_out, b_out], w_o_mix[0], (x_p, x_s), "mix_out")
    h = _ffn(h, norm_ffn[0:1], w_ffn_gate[0], w_ffn_up[0], w_ffn_down[0])

    pool_state_prompt = jnp.stack(
        [y[2, (b + 1) * SEQ - POOL_HIST:(b + 1) * SEQ] for b in range(BATCH)])[None]
    z_s = y[2, T_P:]
    pool_state_sample = jnp.concatenate([state_pool[0][:, 1:], z_s[:, None, :]], axis=1)[None]
    chunk_v_sample = y[1, T_P:].reshape(1, T_S, 1, A_WIDTH)

    cos_t, sin_t = _rope_tables()
    w_dqkv_pad = jnp.pad(w_dqkv[0], ((0, 0), (0, 128 - QK_ROPE)))
    cq, ckv, pe, lat_p, lat_s = _dqkv(h, norm_mix[1:2], w_dqkv_pad, g_q, g_kv, cos_t, sin_t)
    wq_pad = jnp.pad(w_uq[0].reshape(Q_LORA, N_HEADS, QK_NOPE + QK_ROPE),
                     ((0, 0), (0, 0), (0, HEAD_PAD - QK_NOPE - QK_ROPE))).reshape(Q_LORA, N_HEADS * HEAD_PAD)
    wk2d = w_uk[0].reshape(KV_LORA, N_HEADS * QK_NOPE)
    wv2d = w_uv[0].reshape(KV_LORA, N_HEADS * V_HEAD)
    q, k, vt = _qkv(cq, ckv, pe, cos_t, sin_t, wq_pad, wk2d, wv2d.T)
    attn = _flash(q, k, vt)

    qlat = _qlat(q, wk2d)
    cache_t = jnp.swapaxes(cache_mla, 2, 3).reshape(-1, LATENT, PAGE_SIZE)
    olat = _decode(page_table.reshape(-1),
                   qlat.reshape(T_S, N_HEADS, KV_LORA),
                   q[T_P:].reshape(T_S, N_HEADS, HEAD_PAD),
                   lat_s,
                   cache_t)
    attn_s = _ouv(olat.reshape(T_S, N_HEADS * KV_LORA), wv2d)
    h = _proj_res([(attn, attn_s)], w_o_attn[0], h, "attn_out")

    mla_rows_prompt = lat_p.reshape(1, BATCH, SEQ, LATENT)
    mla_rows_sample = lat_s.reshape(1, T_S, 1, LATENT)

    w_router_pad = jnp.pad(w_router[0], ((0, 0), (0, 128 - N_EXPERTS)))
    meta = _router(h, norm_ffn[1:2], w_router_pad)
    experts = meta[:, :TOP_K].astype(jnp.int32)
    te, rows, ybase, src, pos = _routing_tables(experts)
    y_sorted = _moe(te, rows, ybase, src, h, norm_ffn[1:2],
                    w_exp_gate[0], w_exp_up[0], w_exp_down[0])
    y_p, y_s = _combine(pos, h, meta, norm_final.reshape(1, D_MODEL), y_sorted)

    return (y_p.reshape(BATCH, SEQ, D_MODEL), y_s.reshape(T_S, 1, D_MODEL),
            pool_state_prompt, pool_state_sample, chunk_v_sample,
            mla_rows_prompt, mla_rows_sample)
```

```python
import functools

import numpy as np
import jax
import jax.numpy as jnp
from jax import lax
from jax.experimental import pallas as pl
from jax.experimental.pallas import tpu as pltpu

D_MODEL = 2048
BATCH = 4
SEQ = 2048
DEC_BATCH = 128
PAST_LEN = 8192
PAGE_SIZE = 128
N_PAGES = PAST_LEN // PAGE_SIZE
A_WIDTH = 1024
A_HEADS = 8
A_HEAD_DIM = 128
CHUNK = 128
B_WIDTH = 1024
POOL_WINDOWS = (2, 4, 8, 16)
B_GROUP_DIM = 256
POOL_HIST = 15
N_HEADS = 16
QK_NOPE = 128
QK_ROPE = 64
V_HEAD = 128
Q_LORA = 512
KV_LORA = 512
LATENT = KV_LORA + QK_ROPE
ROPE_THETA = 10000.0
ATTN_SCALE = (QK_NOPE + QK_ROPE) ** -0.5
LOG2_E = float(np.log2(np.e))
D_FF = 7168
N_EXPERTS = 8
TOP_K = 2
EPS = 1e-6

T_P = BATCH * SEQ
T_S = DEC_BATCH
T = T_P + T_S

BF = jnp.bfloat16
F32 = jnp.float32

VMEM_LIMIT_BYTES = 56 * 1024 * 1024

TM = 640
N_TM = T // TM
TAIL_P = T_P - (N_TM - 1) * TM
TM_FFN = 1040
TF = 256
TF_FFN = 512
HEAD_PAD = 256
TQ = 512
N_QT = SEQ // TQ
PAGES_PER_STEP = 32
N_KV_STEPS = N_PAGES // PAGES_PER_STEP
KV_SLOTS = 3
MOE_CH = 128
MOE_BLOCKS = (4, 2, 1)
MOE_TM = 18 * MOE_CH
DMA_UNROLL = 8
MOE_NT = (T * TOP_K) // MOE_TM + N_EXPERTS
MOE_ROWS = -(-(T * TOP_K + MOE_NT * (MOE_CH - 1)) // MOE_CH) * MOE_CH
TC = 128
N_TC = T // TC
N_TC_P = T_P // TC


def _cparams(sem):
    return pltpu.CompilerParams(dimension_semantics=sem, vmem_limit_bytes=VMEM_LIMIT_BYTES)


def _rms(x, g):
    return x * lax.rsqrt(jnp.mean(x * x, axis=-1, keepdims=True) + EPS) * g


def _gelu(x):
    return 0.5 * x * (1.0 + lax.erf(x * np.float32(np.sqrt(0.5))))


def _bdot(a, b):
    return jnp.dot(a, b, preferred_element_type=F32)


def _bdot_nt(a, b):
    return lax.dot_general(a, b, (((1,), (1,)), ((), ())), preferred_element_type=F32)


def _rope128(x, c, s):
    swapped = pltpu.roll(x, 96, axis=1) + pltpu.roll(x, 32, axis=1)
    return x * c + swapped * s


def _inproj_body(xp_ref, xs_ref, g_ref, w_ref, gv_ref, o_ref):
    j = pl.program_id(0)
    w = w_ref[...].astype(BF)

    def project(x_rows, out_rows):
        y = _bdot(_rms(x_rows, g_ref[...]).astype(BF), w)

        @pl.when(j == 0)
        def _():
            o_ref[0, out_rows, :] = _gelu(y)

        @pl.when(j == 1)
        def _():
            o_ref[0, out_rows, :] = _rms(_gelu(y), gv_ref[...])

        @pl.when(j == 2)
        def _():
            o_ref[0, out_rows, :] = y

    project(xp_ref[...], slice(None))

    @pl.when(pl.program_id(1) == N_TM - 1)
    def _():
        project(xs_ref[...], slice(TAIL_P, TM))


def _inproj(x_p, x_s, g, w, gv):
    return pl.pallas_call(
        _inproj_body,
        grid=(3, N_TM),
        in_specs=[
            pl.BlockSpec((TM, D_MODEL), lambda j, i: (i, 0)),
            pl.BlockSpec((T_S, D_MODEL), lambda j, i: (0, 0)),
            pl.BlockSpec((1, D_MODEL), lambda j, i: (0, 0)),
            pl.BlockSpec((D_MODEL, A_WIDTH), lambda j, i: (0, j)),
            pl.BlockSpec((1, A_WIDTH), lambda j, i: (0, 0)),
        ],
        out_specs=pl.BlockSpec((1, TM, A_WIDTH), lambda j, i: (j, i, 0)),
        out_shape=jax.ShapeDtypeStruct((3, T, A_WIDTH), F32),
        compiler_params=_cparams(("arbitrary", "arbitrary")),
        name="inproj",
    )(x_p, x_s, g, w, gv)


def _gate_body(u_ref, v_ref, ws_ref, bb_ref, a_ref):
    assert TAIL_P % CHUNK == 0 and T_S == CHUNK
    n_chunks = TM // CHUNK
    last_tile = pl.program_id(0) == N_TM - 1
    row = lax.broadcasted_iota(jnp.int32, (CHUNK, CHUNK), 0)
    col = lax.broadcasted_iota(jnp.int32, (CHUNK, CHUNK), 1)
    for h in range(A_HEADS):
        w = ws_ref[h]
        b = bb_ref[h]
        w_chunk = jnp.where(col <= row, w, 0.0)
        w_single = jnp.where(col == row, jnp.broadcast_to(w[0:1, 0:1], (CHUNK, CHUNK)), 0.0)
        w_prompt = w_chunk.astype(BF)
        w_tail = jnp.where(last_tile, w_single, w_chunk).astype(BF)
        b_tail = jnp.where(last_tile, jnp.broadcast_to(b[0:1, :], (CHUNK, CHUNK)), b)
        sl = slice(h * A_HEAD_DIM, (h + 1) * A_HEAD_DIM)
        for c in range(n_chunks):
            rows = slice(c * CHUNK, (c + 1) * CHUNK)
            tail = c == n_chunks - 1
            mixed = _bdot(w_tail if tail else w_prompt, v_ref[0, rows, sl].astype(BF))
            mixed = mixed + (b_tail if tail else b)
            a_ref[rows, sl] = (u_ref[0, rows, sl] * mixed).astype(BF)


def _gate(y, w_s, b_bcast):
    return pl.pallas_call(
        _gate_body,
        grid=(N_TM,),
        in_specs=[
            pl.BlockSpec((1, TM, A_WIDTH), lambda i: (0, i, 0)),
            pl.BlockSpec((1, TM, A_WIDTH), lambda i: (1, i, 0)),
            pl.BlockSpec((A_HEADS, CHUNK, CHUNK), lambda i: (0, 0, 0)),
            pl.BlockSpec((A_HEADS, CHUNK, CHUNK), lambda i: (0, 0, 0)),
        ],
        out_specs=pl.BlockSpec((TM, A_WIDTH), lambda i: (i, 0)),
        out_shape=jax.ShapeDtypeStruct((T, A_WIDTH), BF),
        compiler_params=_cparams(("arbitrary",)),
        name="gate",
    )(y, y, w_s, b_bcast)


def _pool_body(z_ref, hist_ref, wp_ref, ps_ref, o_ref):
    b = pl.program_id(0)

    def project(pooled, g):
        out = _bdot(pooled.astype(BF), wp_ref[g].astype(BF)) * ps_ref[g:g + 1, :]
        return out.astype(BF)

    @pl.when(b < BATCH)
    def _():
        row = lax.broadcasted_iota(jnp.int32, (SEQ, B_GROUP_DIM), 0)
        for g, w in enumerate(POOL_WINDOWS):
            sl = slice(g * B_GROUP_DIM, (g + 1) * B_GROUP_DIM)
            z = z_ref[0, :, sl]
            s = z
            k = 1
            while k < w:
                s = s + jnp.where(row >= k, pltpu.roll(s, k, axis=0), 0.0)
                k *= 2
            cnt = jnp.minimum(row + 1, w).astype(F32)
            o_ref[:, sl] = project(s / cnt - z, g)

    @pl.when(b == BATCH)
    def _():
        for g, w in enumerate(POOL_WINDOWS):
            sl = slice(g * B_GROUP_DIM, (g + 1) * B_GROUP_DIM)
            z = z_ref[0, 0:T_S, sl]
            s = z
            for k in range(POOL_HIST + 1 - w, POOL_HIST):
                s = s + hist_ref[:, k * B_WIDTH + g * B_GROUP_DIM:k * B_WIDTH + (g + 1) * B_GROUP_DIM]
            o_ref[0:T_S, sl] = project(s / np.float32(w) - z, g)


def _pool(y, hist2d, w_pool, pool_scale):
    return pl.pallas_call(
        _pool_body,
        grid=(BATCH + 1,),
        in_specs=[
            pl.BlockSpec((1, SEQ, B_WIDTH), lambda b: (2, b, 0)),
            pl.BlockSpec((T_S, POOL_HIST * B_WIDTH), lambda b: (0, 0)),
            pl.BlockSpec((len(POOL_WINDOWS), B_GROUP_DIM, B_GROUP_DIM), lambda b: (0, 0, 0)),
            pl.BlockSpec((len(POOL_WINDOWS), B_GROUP_DIM), lambda b: (0, 0)),
        ],
        out_specs=pl.BlockSpec((SEQ, B_WIDTH), lambda b: (b, 0)),
        out_shape=jax.ShapeDtypeStruct((T, B_WIDTH), BF),
        compiler_params=_cparams(("arbitrary",)),
        name="pool",
    )(y, hist2d, w_pool, pool_scale)


def _proj_res_body(*refs, split):
    refs = list(refs)
    parts = []
    for is_pair in split:
        parts.append((refs.pop(0), refs.pop(0) if is_pair else None))
    w_ref, o_ref = refs
    acts, (res_ref, res_tail_ref) = parts[:-1], parts[-1]
    w = w_ref[...].astype(BF)
    acc = res_ref[...]
    off = 0
    for a_ref, _ in acts:
        k = a_ref.shape[1]
        acc = acc + _bdot(a_ref[...], w[off:off + k, :])
        off += k
    o_ref[...] = acc

    if any(split):
        @pl.when(pl.program_id(1) == N_TM - 1)
        def _():
            tail = res_ref[TAIL_P:, :] if res_tail_ref is None else res_tail_ref[...]
            off = 0
            for a_ref, a_tail_ref in acts:
                k = a_ref.shape[1]
                a = a_ref[TAIL_P:, :] if a_tail_ref is None else a_tail_ref[...]
                tail = tail + _bdot(a, w[off:off + k, :])
                off += k
            o_ref[TAIL_P:, :] = tail


def _proj_res(acts, w, res, name):
    tn = 1024
    k_total = w.shape[0]
    operands, in_specs, split = [], [], []
    for n, x in enumerate(list(acts) + [res]):
        is_res = n == len(acts)
        pair = isinstance(x, tuple)
        split.append(pair)
        for part, rows in zip(x if pair else (x,), (TM, T_S)):
            cols = tn if is_res else part.shape[1]
            operands.append(part)
            if rows == TM:
                in_specs.append(pl.BlockSpec((TM, cols), (lambda j, i: (i, j)) if is_res
                                             else (lambda j, i: (i, 0))))
            else:
                in_specs.append(pl.BlockSpec((T_S, cols), (lambda j, i: (0, j)) if is_res
                                             else (lambda j, i: (0, 0))))
    return pl.pallas_call(
        functools.partial(_proj_res_body, split=tuple(split)),
        grid=(D_MODEL // tn, N_TM),
        in_specs=in_specs + [pl.BlockSpec((k_total, tn), lambda j, i: (0, j))],
        out_specs=pl.BlockSpec((TM, tn), lambda j, i: (i, j)),
        out_shape=jax.ShapeDtypeStruct((T, D_MODEL), F32),
        compiler_params=_cparams(("arbitrary", "arbitrary")),
        name=name,
    )(*operands, w)


def _ffn_body(x_hbm, g_ref, wg_ref, wu_ref, wd_ref, o_hbm, acc_ref, xn_ref, sem):
    i = pl.program_id(0)
    f = pl.program_id(1)
    rows = pl.ds(i * TM_FFN, TM_FFN)

    @pl.when(f == 0)
    def _():
        cp = pltpu.make_async_copy(x_hbm.at[rows], acc_ref, sem)
        cp.start()
        cp.wait()
        xn_ref[...] = _rms(acc_ref[...], g_ref[...]).astype(BF)

    xn = xn_ref[...]
    gate = _bdot(xn, wg_ref[...].astype(BF))
    up = _bdot(xn, wu_ref[...].astype(BF))
    act = (gate * jax.nn.sigmoid(gate) * up).astype(BF)
    acc_ref[...] += _bdot(act, wd_ref[...].astype(BF))

    @pl.when(f == pl.num_programs(1) - 1)
    def _():
        cp = pltpu.make_async_copy(acc_ref, o_hbm.at[rows], sem)
        cp.start()
        cp.wait()


def _ffn(x, g, wg, wu, wd):
    return pl.pallas_call(
        _ffn_body,
        grid=(T // TM_FFN, D_FF // TF_FFN),
        in_specs=[
            pl.BlockSpec(memory_space=pl.ANY),
            pl.BlockSpec((1, D_MODEL), lambda i, f: (0, 0)),
            pl.BlockSpec((D_MODEL, TF_FFN), lambda i, f: (0, f)),
            pl.BlockSpec((D_MODEL, TF_FFN), lambda i, f: (0, f)),
            pl.BlockSpec((TF_FFN, D_MODEL), lambda i, f: (f, 0)),
        ],
        out_specs=pl.BlockSpec(memory_space=pl.ANY),
        out_shape=jax.ShapeDtypeStruct((T, D_MODEL), F32),
        scratch_shapes=[pltpu.VMEM((TM_FFN, D_MODEL), F32),
                        pltpu.VMEM((TM_FFN, D_MODEL), BF),
                        pltpu.SemaphoreType.DMA(())],
        compiler_params=_cparams(("arbitrary", "arbitrary")),
        name="ffn",
    )(x, g, wg, wu, wd)


def _dqkv_body(x_ref, g_ref, w_ref, gq_ref, gkv_ref, cos_ref, sin_ref,
               cq_ref, ckv_ref, pe_ref, lat_ref, lat_s_ref):
    xn = _rms(x_ref[...], g_ref[...]).astype(BF)
    d = _bdot(xn, w_ref[...].astype(BF))
    cq_ref[...] = _rms(d[:, :Q_LORA], gq_ref[...]).astype(BF)
    ckv = _rms(d[:, Q_LORA:Q_LORA + KV_LORA], gkv_ref[...])
    pe = _rope128(d[:, Q_LORA + KV_LORA:], cos_ref[...], sin_ref[...])
    ckv_ref[...] = ckv.astype(BF)
    pe_ref[...] = pe.astype(BF)
    lat_ref[:, :KV_LORA] = ckv
    lat_ref[:, KV_LORA:] = pe[:, :QK_ROPE]

    @pl.when(pl.program_id(0) == N_TM - 1)
    def _():
        lat_s_ref[:, :KV_LORA] = ckv[TAIL_P:, :]
        lat_s_ref[:, KV_LORA:] = pe[TAIL_P:, :QK_ROPE]


def _dqkv(x, g, w_pad, gq, gkv, cos_t, sin_t):
    n = w_pad.shape[1]
    row = lambda i: (i, 0)
    fix = lambda i: (0, 0)
    return pl.pallas_call(
        _dqkv_body,
        grid=(N_TM,),
        in_specs=[
            pl.BlockSpec((TM, D_MODEL), row),
            pl.BlockSpec((1, D_MODEL), fix),
            pl.BlockSpec((D_MODEL, n), fix),
            pl.BlockSpec((1, Q_LORA), fix),
            pl.BlockSpec((1, KV_LORA), fix),
            pl.BlockSpec((TM, 128), row),
            pl.BlockSpec((TM, 128), row),
        ],
        out_specs=[
            pl.BlockSpec((TM, Q_LORA), row),
            pl.BlockSpec((TM, KV_LORA), row),
            pl.BlockSpec((TM, 128), row),
            pl.BlockSpec((TM, LATENT), row),
            pl.BlockSpec((T_S, LATENT), fix),
        ],
        out_shape=[
            jax.ShapeDtypeStruct((T, Q_LORA), BF),
            jax.ShapeDtypeStruct((T, KV_LORA), BF),
            jax.ShapeDtypeStruct((T, 128), BF),
            jax.ShapeDtypeStruct((T_P, LATENT), F32),
            jax.ShapeDtypeStruct((T_S, LATENT), F32),
        ],
        compiler_params=_cparams(("arbitrary",)),
        name="dqkv",
    )(x, g, w_pad, gq, gkv, cos_t, sin_t)


HG = 4


def _qkv_body(cq_ref, ckv_ref, pe_ref, cos_ref, sin_ref, wq_ref, wk_ref, wvt_ref,
              q_ref, k_ref, vt_ref):
    q = _bdot(cq_ref[...], wq_ref[...].astype(BF)) * np.float32(ATTN_SCALE * LOG2_E)
    ckv = ckv_ref[...]
    kn = _bdot(ckv, wk_ref[...].astype(BF))
    pe = pe_ref[...]
    c = cos_ref[...]
    s = sin_ref[...]
    for h in range(HG):
        o = h * HEAD_PAD
        q_ref[:, o:o + QK_NOPE] = q[:, o:o + QK_NOPE].astype(BF)
        q_ref[:, o + QK_NOPE:o + HEAD_PAD] = _rope128(q[:, o + QK_NOPE:o + HEAD_PAD], c, s).astype(BF)
        k_ref[:, o:o + QK_NOPE] = kn[:, h * QK_NOPE:(h + 1) * QK_NOPE].astype(BF)
        k_ref[:, o + QK_NOPE:o + HEAD_PAD] = pe
    vt_ref[...] = _bdot_nt(wvt_ref[...].astype(BF), ckv).astype(BF)


def _qkv(cq, ckv, pe, cos_t, sin_t, wq_pad, wk, wv_t):
    row = lambda j, i: (i, 0)
    col = lambda j, i: (0, j)
    out = lambda j, i: (i, j)
    return pl.pallas_call(
        _qkv_body,
        grid=(N_HEADS // HG, N_TM),
        in_specs=[
            pl.BlockSpec((TM, Q_LORA), row),
            pl.BlockSpec((TM, KV_LORA), row),
            pl.BlockSpec((TM, 128), row),
            pl.BlockSpec((TM, 128), row),
            pl.BlockSpec((TM, 128), row),
            pl.BlockSpec((Q_LORA, HG * HEAD_PAD), col),
            pl.BlockSpec((KV_LORA, HG * QK_NOPE), col),
            pl.BlockSpec((HG * V_HEAD, KV_LORA), lambda j, i: (j, 0)),
        ],
        out_specs=[
            pl.BlockSpec((TM, HG * HEAD_PAD), out),
            pl.BlockSpec((TM, HG * HEAD_PAD), out),
            pl.BlockSpec((HG * V_HEAD, TM), lambda j, i: (j, i)),
        ],
        out_shape=[
            jax.ShapeDtypeStruct((T, N_HEADS * HEAD_PAD), BF),
            jax.ShapeDtypeStruct((T, N_HEADS * HEAD_PAD), BF),
            jax.ShapeDtypeStruct((N_HEADS * V_HEAD, T), BF),
        ],
        compiler_params=_cparams(("arbitrary", "arbitrary")),
        name="qkv",
    )(cq, ckv, pe, cos_t, sin_t, wq_pad, wk, wv_t)


_PAIRS = [(qi, ki) for qi in range(N_QT) for ki in range(qi + 1)]


def _flash_body(qi_tab, ki_tab, q_ref, k_ref, vt_ref, o_ref, m_ref, l_ref, acc_ref):
    p = pl.program_id(1)
    qi = qi_tab[p]
    ki = ki_tab[p]

    @pl.when(ki == 0)
    def _():
        m_ref[...] = jnp.full(m_ref.shape, -jnp.inf, F32)
        l_ref[...] = jnp.zeros(l_ref.shape, F32)
        acc_ref[...] = jnp.zeros(acc_ref.shape, F32)

    def all_heads(on_diagonal):
        if on_diagonal:
            key = lax.broadcasted_iota(jnp.int32, (TQ, TQ), 0)
            qry = lax.broadcasted_iota(jnp.int32, (TQ, TQ), 1)
            visible = key <= qry
        for h in range(N_HEADS):
            qk = slice(h * HEAD_PAD, (h + 1) * HEAD_PAD)
            vs = slice(h * V_HEAD, (h + 1) * V_HEAD)
            st = _bdot_nt(k_ref[:, qk], q_ref[:, qk])
            if on_diagonal:
                st = jnp.where(visible, st, -jnp.inf)
            m_prev = m_ref[h]
            m_new = jnp.maximum(m_prev, jnp.max(st, axis=0, keepdims=True))
            alpha = jnp.exp2(m_prev - m_new)
            pt = jnp.exp2(st - m_new)
            l_ref[h] = alpha * l_ref[h] + jnp.sum(pt, axis=0, keepdims=True)
            acc_ref[vs, :] = alpha * acc_ref[vs, :] + _bdot(vt_ref[vs, :], pt.astype(BF))
            m_ref[h] = m_new

    @pl.when(ki < qi)
    def _():
        all_heads(False)

    @pl.when(ki == qi)
    def _():
        all_heads(True)
        for h in range(N_HEADS):
            vs = slice(h * V_HEAD, (h + 1) * V_HEAD)
            o_ref[:, vs] = (acc_ref[vs, :] / l_ref[h]).T.astype(BF)


def _flash(q, k, vt):
    qi_tab = jnp.asarray(np.array([p[0] for p in _PAIRS], np.int32))
    ki_tab = jnp.asarray(np.array([p[1] for p in _PAIRS], np.int32))
    grid_spec = pltpu.PrefetchScalarGridSpec(
        num_scalar_prefetch=2,
        grid=(BATCH, len(_PAIRS)),
        in_specs=[
            pl.BlockSpec((TQ, N_HEADS * HEAD_PAD), lambda b, p, qt, kt: (b * N_QT + qt[p], 0)),
            pl.BlockSpec((TQ, N_HEADS * HEAD_PAD), lambda b, p, qt, kt: (b * N_QT + kt[p], 0)),
            pl.BlockSpec((N_HEADS * V_HEAD, TQ), lambda b, p, qt, kt: (0, b * N_QT + kt[p])),
        ],
        out_specs=pl.BlockSpec((TQ, N_HEADS * V_HEAD), lambda b, p, qt, kt: (b * N_QT + qt[p], 0)),
        scratch_shapes=[
            pltpu.VMEM((N_HEADS, 1, TQ), F32),
            pltpu.VMEM((N_HEADS, 1, TQ), F32),
            pltpu.VMEM((N_HEADS * V_HEAD, TQ), F32),
        ],
    )
    return pl.pallas_call(
        _flash_body,
        grid_spec=grid_spec,
        out_shape=jax.ShapeDtypeStruct((T_P, N_HEADS * V_HEAD), BF),
        compiler_params=_cparams(("arbitrary", "arbitrary")),
        name="flash",
    )(qi_tab, ki_tab, q, k, vt)


def _qlat_body(q_ref, wk_ref, o_ref):
    o_ref[...] = _bdot_nt(q_ref[:, :QK_NOPE], wk_ref[...].astype(BF)).astype(BF)


def _qlat(q, wk):
    return pl.pallas_call(
        _qlat_body,
        grid=(N_HEADS,),
        in_specs=[
            pl.BlockSpec((T_S, HEAD_PAD), lambda h: (T_P // T_S, h)),
            pl.BlockSpec((KV_LORA, QK_NOPE), lambda h: (0, h)),
        ],
        out_specs=pl.BlockSpec((T_S, KV_LORA), lambda h: (0, h)),
        out_shape=jax.ShapeDtypeStruct((T_S, N_HEADS * KV_LORA), BF),
        compiler_params=_cparams(("arbitrary",)),
        name="qlat",
    )(q, wk)


def _decode_body(pt_ref, qlat_ref, q_ref, lat_ref, cache_hbm, o_ref,
                 buf, kb_ref, sems, m_ref, l_ref, acc_ref):
    b = pl.program_id(0)
    c = pl.program_id(1)
    step = b * N_KV_STEPS + c
    n_steps = DEC_BATCH * N_KV_STEPS
    slot = step % KV_SLOTS

    def page_copy(st, j, sl):
        pg = pt_ref[st * PAGES_PER_STEP + j]
        return pltpu.make_async_copy(cache_hbm.at[pg], buf.at[sl, j], sems.at[sl])

    def start_fetch(st, sl):
        for j in range(PAGES_PER_STEP):
            page_copy(st, j, sl).start()

    @pl.when(step == 0)
    def _():
        for s0 in range(KV_SLOTS - 1):
            start_fetch(s0, s0)

    ahead = step + (KV_SLOTS - 1)

    @pl.when(ahead < n_steps)
    def _():
        start_fetch(ahead, ahead % KV_SLOTS)

    qlat = qlat_ref[0]
    qpe = q_ref[0, :, QK_NOPE:QK_NOPE + QK_ROPE]

    @pl.when(c == 0)
    def _():
        lat = lat_ref[pl.ds(b, 1), :]
        s_self = (jnp.sum(qlat.astype(F32) * lat[:, :KV_LORA], axis=-1, keepdims=True)
                  + jnp.sum(qpe.astype(F32) * lat[:, KV_LORA:], axis=-1, keepdims=True))
        m_ref[...] = s_self
        l_ref[...] = jnp.ones(l_ref.shape, F32)
        acc_ref[...] = jnp.broadcast_to(lat[:, :KV_LORA], acc_ref.shape)

    for j in range(PAGES_PER_STEP):
        page_copy(step, j, slot).wait()

    s_parts = []
    for j in range(PAGES_PER_STEP):
        kb_ref[j] = buf[slot, j].astype(BF)
        s_parts.append(_bdot(qlat, kb_ref[j, :KV_LORA, :]) + _bdot(qpe, kb_ref[j, KV_LORA:, :]))
    s = jnp.concatenate(s_parts, axis=1)
    m_prev = m_ref[...]
    m_new = jnp.maximum(m_prev, jnp.max(s, axis=-1, keepdims=True))
    alpha = jnp.exp2(m_prev - m_new)
    pexp = jnp.exp2(s - m_new)
    l_ref[...] = alpha * l_ref[...] + jnp.sum(pexp, axis=-1, keepdims=True)
    pb = pexp.astype(BF)
    pv = jnp.zeros(acc_ref.shape, F32)
    for j in range(PAGES_PER_STEP):
        pv = pv + _bdot_nt(pb[:, j * PAGE_SIZE:(j + 1) * PAGE_SIZE], kb_ref[j, :KV_LORA, :])
    acc_ref[...] = alpha * acc_ref[...] + pv
    m_ref[...] = m_new

    @pl.when(c == N_KV_STEPS - 1)
    def _():
        o_ref[0] = acc_ref[...] / l_ref[...]


def _decode(page_table_flat, qlat3, q3, lat_s, cache_t):
    grid_spec = pltpu.PrefetchScalarGridSpec(
        num_scalar_prefetch=1,
        grid=(DEC_BATCH, N_KV_STEPS),
        in_specs=[
            pl.BlockSpec((1, N_HEADS, KV_LORA), lambda b, c, pt: (b, 0, 0)),
            pl.BlockSpec((1, N_HEADS, HEAD_PAD), lambda b, c, pt: (b, 0, 0)),
            pl.BlockSpec((T_S, LATENT), lambda b, c, pt: (0, 0)),
            pl.BlockSpec(memory_space=pl.ANY),
        ],
        out_specs=pl.BlockSpec((1, N_HEADS, KV_LORA), lambda b, c, pt: (b, 0, 0)),
        scratch_shapes=[
            pltpu.VMEM((KV_SLOTS, PAGES_PER_STEP, LATENT, PAGE_SIZE), F32),
            pltpu.VMEM((PAGES_PER_STEP, LATENT, PAGE_SIZE), BF),
            pltpu.SemaphoreType.DMA((KV_SLOTS,)),
            pltpu.VMEM((N_HEADS, 1), F32),
            pltpu.VMEM((N_HEADS, 1), F32),
            pltpu.VMEM((N_HEADS, KV_LORA), F32),
        ],
    )
    return pl.pallas_call(
        _decode_body,
        grid_spec=grid_spec,
        out_shape=jax.ShapeDtypeStruct((DEC_BATCH, N_HEADS, KV_LORA), F32),
        compiler_params=_cparams(("arbitrary", "arbitrary")),
        name="decode",
    )(page_table_flat, qlat3, q3, lat_s, cache_t)


def _ouv_body(o_ref, wv_ref, out_ref):
    out_ref[...] = _bdot(o_ref[...].astype(BF), wv_ref[...].astype(BF)).astype(BF)


def _ouv(olat2d, wv):
    return pl.pallas_call(
        _ouv_body,
        grid=(N_HEADS,),
        in_specs=[
            pl.BlockSpec((T_S, KV_LORA), lambda h: (0, h)),
            pl.BlockSpec((KV_LORA, V_HEAD), lambda h: (0, h)),
        ],
        out_specs=pl.BlockSpec((T_S, V_HEAD), lambda h: (0, h)),
        out_shape=jax.ShapeDtypeStruct((T_S, N_HEADS * V_HEAD), BF),
        compiler_params=_cparams(("arbitrary",)),
        name="ouv",
    )(olat2d, wv)


def _router_body(x_ref, g_ref, w_ref, meta_ref):
    xn = _rms(x_ref[...], g_ref[...])
    w = w_ref[...]
    xh = xn.astype(BF)
    xl = (xn - xh.astype(F32)).astype(BF)
    wh = w.astype(BF)
    wl = (w - wh.astype(F32)).astype(BF)
    logits = _bdot(xh, wh) + (_bdot(xh, wl) + _bdot(xl, wh))
    lane = lax.broadcasted_iota(jnp.int32, logits.shape, 1).astype(F32)
    lg = jnp.where(lane < N_EXPERTS, logits, -jnp.inf)
    t1 = jnp.max(lg, axis=-1, keepdims=True)
    i1 = jnp.min(jnp.where(lg == t1, lane, 128.0), axis=-1, keepdims=True)
    lg2 = jnp.where(lane == i1, -jnp.inf, lg)
    t2 = jnp.max(lg2, axis=-1, keepdims=True)
    i2 = jnp.min(jnp.where(lg2 == t2, lane, 128.0), axis=-1, keepdims=True)
    e = jnp.exp(t2 - t1)
    g1 = 1.0 / (1.0 + e)
    g2 = e / (1.0 + e)
    meta = jnp.where(lane == 0, i1, jnp.where(lane == 1, i2, jnp.where(lane == 2, g1, g2)))
    meta_ref[...] = meta


def _router(x, g, w_pad):
    return pl.pallas_call(
        _router_body,
        grid=(N_TM,),
        in_specs=[
            pl.BlockSpec((TM, D_MODEL), lambda i: (i, 0)),
            pl.BlockSpec((1, D_MODEL), lambda i: (0, 0)),
            pl.BlockSpec((D_MODEL, 128), lambda i: (0, 0)),
        ],
        out_specs=pl.BlockSpec((TM, 128), lambda i: (i, 0)),
        out_shape=jax.ShapeDtypeStruct((T, 128), F32),
        compiler_params=_cparams(("arbitrary",)),
        name="router",
    )(x, g, w_pad)


N_FF = D_FF // TF
N_CH = MOE_TM // MOE_CH


def _moe_body(te_ref, rows_ref, ybase_ref, src_ref, h_hbm, g_ref, wg_ref, wu_ref, wd_ref, y_hbm,
              acc_ref, xn_ref, wgb_ref, wub_ref, wdb_ref, zero_ref, gsems, wsem):
    t = pl.program_id(0)
    f = pl.program_id(1)
    nch = (rows_ref[t] + MOE_CH - 1) // MOE_CH
    big = MOE_BLOCKS[0]
    nbig = nch // big
    base = pl.multiple_of(ybase_ref[t], MOE_CH)

    def remainders():
        out = []
        ch0 = nbig * big
        left = nch - ch0
        for size in MOE_BLOCKS[1:]:
            present = left >= size
            out.append((ch0, size, present))
            ch0 = ch0 + jnp.where(present, size, 0)
            left = left - jnp.where(present, size, 0)
        return out

    def issue_gather(ch):
        def body(r0, carry):
            for j in range(DMA_UNROLL):
                r = ch * MOE_CH + r0 * DMA_UNROLL + j
                pltpu.make_async_copy(h_hbm.at[pl.ds(src_ref[base + r], 1)],
                                      acc_ref.at[pl.ds(r, 1)], gsems.at[ch]).start()
            return carry
        lax.fori_loop(0, MOE_CH // DMA_UNROLL, body, 0)

    def wait_gather(ch):
        pltpu.make_async_copy(h_hbm.at[pl.ds(0, MOE_CH)],
                              acc_ref.at[pl.ds(ch * MOE_CH, MOE_CH)], gsems.at[ch]).wait()

    def writeback(row0, nr):
        return pltpu.make_async_copy(acc_ref.at[pl.ds(row0, nr)],
                                     y_hbm.at[pl.ds(pl.multiple_of(base + row0, MOE_CH), nr)], wsem)

    def cast_weights():
        wg = wg_ref[0].astype(BF)
        wu = wu_ref[0].astype(BF)
        wd = wd_ref[0].astype(BF)
        wgb_ref[...] = wg
        wub_ref[...] = wu
        wdb_ref[...] = wd
        return wg, wu, wd

    def block(ch0, nr, first, last, cast=False):
        row0 = pl.multiple_of(ch0 * MOE_CH, MOE_CH)
        rs = pl.ds(row0, nr)
        if first:
            for c in range(nr // MOE_CH):
                wait_gather(ch0 + c)
            xn_ref[rs, :] = _rms(acc_ref[rs, :], g_ref[...]).astype(BF)
        xn = xn_ref[rs, :]
        wg, wu, wd = cast_weights() if cast else (wgb_ref[...], wub_ref[...], wdb_ref[...])
        gate = _bdot(xn, wg)
        up = _bdot(xn, wu)
        act = (gate * jax.nn.sigmoid(gate) * up).astype(BF)
        out = _bdot(act, wd)
        if first:
            acc_ref[rs, :] = out
        else:
            acc_ref[rs, :] += out
        if last:
            writeback(row0, nr).start()

    def sweep(first, last):
        if first:
            lax.fori_loop(0, nch, lambda ch, c: (issue_gather(ch), c)[1], 0)
        @pl.when(nbig > 0)
        def _():
            block(0, big * MOE_CH, first, last, cast=True)

        @pl.when(nbig == 0)
        def _():
            cast_weights()

        lax.fori_loop(1, nbig, lambda b, c: (block(big * b, big * MOE_CH, first, last), c)[1], 0)
        rems = remainders()
        for ch0, size, present in rems:
            @pl.when(present)
            def _():
                block(ch0, size * MOE_CH, first, last)

        if last:
            lax.fori_loop(0, nbig, lambda b, c: (writeback(0, big * MOE_CH).wait(), c)[1], 0)
            for ch0, size, present in rems:
                @pl.when(present)
                def _():
                    writeback(0, size * MOE_CH).wait()

    @pl.when(nch > 0)
    def _():
        @pl.when(f == 0)
        def _():
            sweep(True, False)

        @pl.when(jnp.logical_and(f > 0, f < N_FF - 1))
        def _():
            sweep(False, False)

        @pl.when(f == N_FF - 1)
        def _():
            sweep(False, True)

    @pl.when(jnp.logical_and(t == MOE_NT - 1, f == N_FF - 1))
    def _():
        used = (base + nch * MOE_CH) // MOE_CH
        zero_ref[...] = jnp.zeros(zero_ref.shape, F32)

        def fill(c):
            return pltpu.make_async_copy(
                zero_ref, y_hbm.at[pl.ds(pl.multiple_of(c * MOE_CH, MOE_CH), MOE_CH)], wsem)

        lax.fori_loop(used, MOE_ROWS // MOE_CH, lambda c, k: (fill(c).start(), k)[1], 0)
        lax.fori_loop(used, MOE_ROWS // MOE_CH, lambda c, k: (fill(0).wait(), k)[1], 0)


def _moe(tile_expert, tile_rows, ybase, src, h, g, wg, wu, wd):
    ff = lambda t, f, nr: jnp.where(nr[t] > 0, f, N_FF - 1)
    grid_spec = pltpu.PrefetchScalarGridSpec(
        num_scalar_prefetch=4,
        grid=(MOE_NT, N_FF),
        in_specs=[
            pl.BlockSpec(memory_space=pl.ANY),
            pl.BlockSpec((1, D_MODEL), lambda t, f, te, nr, yb, sr: (0, 0)),
            pl.BlockSpec((1, D_MODEL, TF), lambda t, f, te, nr, yb, sr: (te[t], 0, ff(t, f, nr))),
            pl.BlockSpec((1, D_MODEL, TF), lambda t, f, te, nr, yb, sr: (te[t], 0, ff(t, f, nr))),
            pl.BlockSpec((1, TF, D_MODEL), lambda t, f, te, nr, yb, sr: (te[t], ff(t, f, nr), 0)),
        ],
        out_specs=pl.BlockSpec(memory_space=pl.ANY),
        scratch_shapes=[
            pltpu.VMEM((MOE_TM, D_MODEL), F32),
            pltpu.VMEM((MOE_TM, D_MODEL), BF),
            pltpu.VMEM((D_MODEL, TF), BF),
            pltpu.VMEM((D_MODEL, TF), BF),
            pltpu.VMEM((TF, D_MODEL), BF),
            pltpu.VMEM((MOE_CH, D_MODEL), F32),
            pltpu.SemaphoreType.DMA((N_CH,)),
            pltpu.SemaphoreType.DMA(()),
        ],
    )
    return pl.pallas_call(
        _moe_body,
        grid_spec=grid_spec,
        out_shape=jax.ShapeDtypeStruct((MOE_ROWS, D_MODEL), F32),
        compiler_params=_cparams(("arbitrary", "arbitrary")),
        name="moe",
    )(tile_expert, tile_rows, ybase, src, h, g, wg, wu, wd)


def _combine_body(pos_ref, h_ref, meta_ref, gf_ref, y_hbm, yp_ref, ys_ref, yb_ref, sems):
    i = pl.program_id(0)
    slot = i % 2

    def fetch(tile, sl):
        rows_per_iter = DMA_UNROLL // TOP_K

        def issue(r0, carry):
            for j in range(rows_per_iter):
                r = r0 * rows_per_iter + j
                for k in range(TOP_K):
                    pltpu.make_async_copy(y_hbm.at[pl.ds(pos_ref[TOP_K * (tile * TC + r) + k], 1)],
                                          yb_ref.at[sl, k, pl.ds(r, 1)], sems.at[sl]).start()
            return carry
        lax.fori_loop(0, TC // rows_per_iter, issue, 0)

    @pl.when(i == 0)
    def _():
        fetch(0, 0)

    @pl.when(i + 1 < N_TC)
    def _():
        fetch(i + 1, 1 - slot)

    for k in range(TOP_K):
        pltpu.make_async_copy(y_hbm.at[pl.ds(0, TC)], yb_ref.at[slot, k], sems.at[slot]).wait()
    meta = meta_ref[...]
    out = h_ref[...] + meta[:, 2:3] * yb_ref[slot, 0] + meta[:, 3:4] * yb_ref[slot, 1]
    out = _rms(out, gf_ref[...])

    @pl.when(i < N_TC_P)
    def _():
        yp_ref[...] = out

    @pl.when(i >= N_TC_P)
    def _():
        ys_ref[...] = out


def _combine(pos, h, meta, gf, y):
    grid_spec = pltpu.PrefetchScalarGridSpec(
        num_scalar_prefetch=1,
        grid=(N_TC,),
        in_specs=[
            pl.BlockSpec((TC, D_MODEL), lambda i, pos: (i, 0)),
            pl.BlockSpec((TC, 128), lambda i, pos: (i, 0)),
            pl.BlockSpec((1, D_MODEL), lambda i, pos: (0, 0)),
            pl.BlockSpec(memory_space=pl.ANY),
        ],
        out_specs=[
            pl.BlockSpec((TC, D_MODEL), lambda i, pos: (jnp.minimum(i, N_TC_P - 1), 0)),
            pl.BlockSpec((T_S, D_MODEL), lambda i, pos: (0, 0)),
        ],
        scratch_shapes=[
            pltpu.VMEM((2, TOP_K, TC, D_MODEL), F32),
            pltpu.SemaphoreType.DMA((2,)),
        ],
    )
    return pl.pallas_call(
        _combine_body,
        grid_spec=grid_spec,
        out_shape=[
            jax.ShapeDtypeStruct((T_P, D_MODEL), F32),
            jax.ShapeDtypeStruct((T_S, D_MODEL), F32),
        ],
        compiler_params=_cparams(("arbitrary",)),
        name="combine",
    )(pos, h, meta, gf, y)


def _routing_tables(experts):
    flat = experts.reshape(-1)
    onehot = (flat[:, None] == jnp.arange(N_EXPERTS, dtype=jnp.int32)[None, :]).astype(jnp.int32)
    csum = jnp.cumsum(onehot, axis=0)
    rank = jnp.take_along_axis(csum, flat[:, None], axis=1)[:, 0] - 1
    counts = csum[-1]
    tiles_per = (counts + MOE_TM - 1) // MOE_TM
    tile_end = jnp.cumsum(tiles_per)
    tile_start = tile_end - tiles_per
    n_used = tile_end[-1]
    tid = jnp.arange(MOE_NT, dtype=jnp.int32)
    last_tile = jnp.maximum(n_used - 1, 0)
    te = jnp.sum(tile_end[None, :] <= jnp.minimum(tid, last_tile)[:, None], axis=1).astype(jnp.int32)
    te = jnp.minimum(te, N_EXPERTS - 1)
    rows = jnp.clip(counts[te] - (tid - tile_start[te]) * MOE_TM, 0, MOE_TM)
    rows = jnp.where(tid < n_used, rows, 0).astype(jnp.int32)
    rows_pad = (rows + MOE_CH - 1) // MOE_CH * MOE_CH
    ybase = (jnp.cumsum(rows_pad) - rows_pad).astype(jnp.int32)
    pos = (ybase[tile_start[flat] + rank // MOE_TM] + rank % MOE_TM).astype(jnp.int32)
    tok = jnp.arange(T * TOP_K, dtype=jnp.int32) // TOP_K
    src = jnp.zeros((MOE_ROWS,), jnp.int32).at[pos].set(tok)
    return te, rows, ybase, src, pos


def _rope_tables():
    half = QK_ROPE // 2
    inv = 1.0 / (ROPE_THETA ** (jnp.arange(half, dtype=F32) * 2.0 / QK_ROPE))
    pos = jnp.concatenate([jnp.tile(jnp.arange(SEQ), BATCH), jnp.full((T_S,), PAST_LEN)])
    ang = pos.astype(F32)[:, None] * inv[None, :]
    cos, sin = jnp.cos(ang), jnp.sin(ang)
    zero = jnp.zeros((T, 128 - QK_ROPE), F32)
    return (jnp.concatenate([cos, cos, zero], axis=1),
            jnp.concatenate([-sin, sin, zero], axis=1))


def kernel(x_prompt, x_sample, state_pool, cache_mla, page_table, norm_mix, norm_ffn, w_in, g_v, w_s, b_s, w_pool, pool_scale, w_o_mix, w_ffn_gate, w_ffn_up, w_ffn_down, w_dqkv, g_q, g_kv, w_uq, w_uk, w_uv, w_o_attn, w_router, w_exp_gate, w_exp_up, w_exp_down, norm_final):
    x_p = x_prompt.reshape(T_P, D_MODEL)
    x_s = x_sample.reshape(T_S, D_MODEL)

    y = _inproj(x_p, x_s, norm_mix[0:1], w_in[0], g_v)
    b_bcast = jnp.broadcast_to(b_s[0][:, :, None], (A_HEADS, CHUNK, CHUNK))
    a_out = _gate(y, w_s[0], b_bcast)
    hist2d = state_pool[0].reshape(T_S, POOL_HIST * B_WIDTH)
    b_out = _pool(y, hist2d, w_pool[0], pool_scale[0])
    h = _proj_res([a_out, b_out], w_o_mix[0], (x_p, x_s), "mix_out")
    h = _ffn(h, norm_ffn[0:1], w_ffn_gate[0], w_ffn_up[0], w_ffn_down[0])

    pool_state_prompt = jnp.stack(
        [y[2, (b + 1) * SEQ - POOL_HIST:(b + 1) * SEQ] for b in range(BATCH)])[None]
    z_s = y[2, T_P:]
    pool_state_sample = jnp.concatenate([state_pool[0][:, 1:], z_s[:, None, :]], axis=1)[None]
    chunk_v_sample = y[1, T_P:].reshape(1, T_S, 1, A_WIDTH)

    cos_t, sin_t = _rope_tables()
    w_dqkv_pad = jnp.pad(w_dqkv[0], ((0, 0), (0, 128 - QK_ROPE)))
    cq, ckv, pe, lat_p, lat_s = _dqkv(h, norm_mix[1:2], w_dqkv_pad, g_q, g_kv, cos_t, sin_t)
    wq_pad = jnp.pad(w_uq[0].reshape(Q_LORA, N_HEADS, QK_NOPE + QK_ROPE),
                     ((0, 0), (0, 0), (0, HEAD_PAD - QK_NOPE - QK_ROPE))).reshape(Q_LORA, N_HEADS * HEAD_PAD)
    wk2d = w_uk[0].reshape(KV_LORA, N_HEADS * QK_NOPE)
    wv2d = w_uv[0].reshape(KV_LORA, N_HEADS * V_HEAD)
    q, k, vt = _qkv(cq, ckv, pe, cos_t, sin_t, wq_pad, wk2d, wv2d.T)
    attn = _flash(q, k, vt)

    qlat = _qlat(q, wk2d)
    cache_t = jnp.swapaxes(cache_mla, 2, 3).reshape(-1, LATENT, PAGE_SIZE)
    olat = _decode(page_table.reshape(-1),
                   qlat.reshape(T_S, N_HEADS, KV_LORA),
                   q[T_P:].reshape(T_S, N_HEADS, HEAD_PAD),
                   lat_s,
                   cache_t)
    attn_s = _ouv(olat.reshape(T_S, N_HEADS * KV_LORA), wv2d)
    h = _proj_res([(attn, attn_s)], w_o_attn[0], h, "attn_out")

    mla_rows_prompt = lat_p.reshape(1, BATCH, SEQ, LATENT)
    mla_rows_sample = lat_s.reshape(1, T_S, 1, LATENT)

    w_router_pad = jnp.pad(w_router[0], ((0, 0), (0, 128 - N_EXPERTS)))
    meta = _router(h, norm_ffn[1:2], w_router_pad)
    experts = meta[:, :TOP_K].astype(jnp.int32)
    te, rows, ybase, src, pos = _routing_tables(experts)
    y_sorted = _moe(te, rows, ybase, src, h, norm_ffn[1:2],
                    w_exp_gate[0], w_exp_up[0], w_exp_down[0])
    y_p, y_s = _combine(pos, h, meta, norm_final.reshape(1, D_MODEL), y_sorted)

    return (y_p.reshape(BATCH, SEQ, D_MODEL), y_s.reshape(T_S, 1, D_MODEL),
            pool_state_prompt, pool_state_sample, chunk_v_sample,
            mla_rows_prompt, mla_rows_sample)
```

```python
import functools

import numpy as np
import jax
import jax.numpy as jnp
from jax import lax
from jax.experimental import pallas as pl
from jax.experimental.pallas import tpu as pltpu

D_MODEL = 2048
BATCH = 4
SEQ = 2048
DEC_BATCH = 128
PAST_LEN = 8192
PAGE_SIZE = 128
N_PAGES = PAST_LEN // PAGE_SIZE
A_WIDTH = 1024
A_HEADS = 8
A_HEAD_DIM = 128
CHUNK = 128
B_WIDTH = 1024
POOL_WINDOWS = (2, 4, 8, 16)
B_GROUP_DIM = 256
POOL_HIST = 15
N_HEADS = 16
QK_NOPE = 128
QK_ROPE = 64
V_HEAD = 128
Q_LORA = 512
KV_LORA = 512
LATENT = KV_LORA + QK_ROPE
ROPE_THETA = 10000.0
ATTN_SCALE = (QK_NOPE + QK_ROPE) ** -0.5
LOG2_E = float(np.log2(np.e))
D_FF = 7168
N_EXPERTS = 8
TOP_K = 2
EPS = 1e-6

T_P = BATCH * SEQ
T_S = DEC_BATCH
T = T_P + T_S

BF = jnp.bfloat16
F32 = jnp.float32

VMEM_LIMIT_BYTES = 56 * 1024 * 1024

TM = 640
N_TM = T // TM
TAIL_P = T_P - (N_TM - 1) * TM
TM_FFN = 1040
TF = 256
TF_FFN = 512
HEAD_PAD = 256
TQ = 512
N_QT = SEQ // TQ
PAGES_PER_STEP = 32
N_KV_STEPS = N_PAGES // PAGES_PER_STEP
KV_SLOTS = 3
MOE_CH = 128
MOE_BLOCKS = (4, 2, 1)
MOE_TM = 18 * MOE_CH
DMA_UNROLL = 8
MOE_NT = (T * TOP_K) // MOE_TM + N_EXPERTS
MOE_ROWS = -(-(T * TOP_K + MOE_NT * (MOE_CH - 1)) // MOE_CH) * MOE_CH
TC = 128
N_TC = T // TC
N_TC_P = T_P // TC


def _cparams(sem):
    return pltpu.CompilerParams(dimension_semantics=sem, vmem_limit_bytes=VMEM_LIMIT_BYTES)


def _rms(x, g):
    return x * lax.rsqrt(jnp.mean(x * x, axis=-1, keepdims=True) + EPS) * g


def _gelu(x):
    return 0.5 * x * (1.0 + lax.erf(x * np.float32(np.sqrt(0.5))))


def _bdot(a, b):
    return jnp.dot(a, b, preferred_element_type=F32)


def _bdot_nt(a, b):
    return lax.dot_general(a, b, (((1,), (1,)), ((), ())), preferred_element_type=F32)


def _rope128(x, c, s):
    swapped = pltpu.roll(x, 96, axis=1) + pltpu.roll(x, 32, axis=1)
    return x * c + swapped * s


def _inproj_body(xp_ref, xs_ref, g_ref, w_ref, gv_ref, o_ref):
    j = pl.program_id(0)
    w = w_ref[...].astype(BF)

    def project(x_rows, out_rows):
        y = _bdot(_rms(x_rows, g_ref[...]).astype(BF), w)

        @pl.when(j == 0)
        def _():
            o_ref[0, out_rows, :] = _gelu(y)

        @pl.when(j == 1)
        def _():
            o_ref[0, out_rows, :] = _rms(_gelu(y), gv_ref[...])

        @pl.when(j == 2)
        def _():
            o_ref[0, out_rows, :] = y

    project(xp_ref[...], slice(None))

    @pl.when(pl.program_id(1) == N_TM - 1)
    def _():
        project(xs_ref[...], slice(TAIL_P, TM))


def _inproj(x_p, x_s, g, w, gv):
    return pl.pallas_call(
        _inproj_body,
        grid=(3, N_TM),
        in_specs=[
            pl.BlockSpec((TM, D_MODEL), lambda j, i: (i, 0)),
            pl.BlockSpec((T_S, D_MODEL), lambda j, i: (0, 0)),
            pl.BlockSpec((1, D_MODEL), lambda j, i: (0, 0)),
            pl.BlockSpec((D_MODEL, A_WIDTH), lambda j, i: (0, j)),
            pl.BlockSpec((1, A_WIDTH), lambda j, i: (0, 0)),
        ],
        out_specs=pl.BlockSpec((1, TM, A_WIDTH), lambda j, i: (j, i, 0)),
        out_shape=jax.ShapeDtypeStruct((3, T, A_WIDTH), F32),
        compiler_params=_cparams(("arbitrary", "arbitrary")),
        name="inproj",
    )(x_p, x_s, g, w, gv)


def _gate_body(u_ref, v_ref, ws_ref, bb_ref, a_ref):
    assert TAIL_P % CHUNK == 0 and T_S == CHUNK
    n_chunks = TM // CHUNK
    last_tile = pl.program_id(0) == N_TM - 1
    row = lax.broadcasted_iota(jnp.int32, (CHUNK, CHUNK), 0)
    col = lax.broadcasted_iota(jnp.int32, (CHUNK, CHUNK), 1)
    for h in range(A_HEADS):
        w = ws_ref[h]
        b = bb_ref[h]
        w_chunk = jnp.where(col <= row, w, 0.0)
        w_single = jnp.where(col == row, jnp.broadcast_to(w[0:1, 0:1], (CHUNK, CHUNK)), 0.0)
        w_prompt = w_chunk.astype(BF)
        w_tail = jnp.where(last_tile, w_single, w_chunk).astype(BF)
        b_tail = jnp.where(last_tile, jnp.broadcast_to(b[0:1, :], (CHUNK, CHUNK)), b)
        sl = slice(h * A_HEAD_DIM, (h + 1) * A_HEAD_DIM)
        for c in range(n_chunks):
            rows = slice(c * CHUNK, (c + 1) * CHUNK)
            tail = c == n_chunks - 1
            mixed = _bdot(w_tail if tail else w_prompt, v_ref[0, rows, sl].astype(BF))
            mixed = mixed + (b_tail if tail else b)
            a_ref[rows, sl] = (u_ref[0, rows, sl] * mixed).astype(BF)


def _gate(y, w_s, b_bcast):
    return pl.pallas_call(
        _gate_body,
        grid=(N_TM,),
        in_specs=[
            pl.BlockSpec((1, TM, A_WIDTH), lambda i: (0, i, 0)),
            pl.BlockSpec((1, TM, A_WIDTH), lambda i: (1, i, 0)),
            pl.BlockSpec((A_HEADS, CHUNK, CHUNK), lambda i: (0, 0, 0)),
            pl.BlockSpec((A_HEADS, CHUNK, CHUNK), lambda i: (0, 0, 0)),
        ],
        out_specs=pl.BlockSpec((TM, A_WIDTH), lambda i: (i, 0)),
        out_shape=jax.ShapeDtypeStruct((T, A_WIDTH), BF),
        compiler_params=_cparams(("arbitrary",)),
        name="gate",
    )(y, y, w_s, b_bcast)


def _pool_body(z_ref, hist_ref, wp_ref, ps_ref, o_ref):
    b = pl.program_id(0)

    def project(pooled, g):
        out = _bdot(pooled.astype(BF), wp_ref[g].astype(BF)) * ps_ref[g:g + 1, :]
        return out.astype(BF)

    @pl.when(b < BATCH)
    def _():
        row = lax.broadcasted_iota(jnp.int32, (SEQ, B_GROUP_DIM), 0)
        for g, w in enumerate(POOL_WINDOWS):
            sl = slice(g * B_GROUP_DIM, (g + 1) * B_GROUP_DIM)
            z = z_ref[0, :, sl]
            s = z
            k = 1
            while k < w:
                s = s + jnp.where(row >= k, pltpu.roll(s, k, axis=0), 0.0)
                k *= 2
            cnt = jnp.minimum(row + 1, w).astype(F32)
            o_ref[:, sl] = project(s / cnt - z, g)

    @pl.when(b == BATCH)
    def _():
        for g, w in enumerate(POOL_WINDOWS):
            sl = slice(g * B_GROUP_DIM, (g + 1) * B_GROUP_DIM)
            z = z_ref[0, 0:T_S, sl]
            s = z
            for k in range(POOL_HIST + 1 - w, POOL_HIST):
                s = s + hist_ref[:, k * B_WIDTH + g * B_GROUP_DIM:k * B_WIDTH + (g + 1) * B_GROUP_DIM]
            o_ref[0:T_S, sl] = project(s / np.float32(w) - z, g)


def _pool(y, hist2d, w_pool, pool_scale):
    return pl.pallas_call(
        _pool_body,
        grid=(BATCH + 1,),
        in_specs=[
            pl.BlockSpec((1, SEQ, B_WIDTH), lambda b: (2, b, 0)),
            pl.BlockSpec((T_S, POOL_HIST * B_WIDTH), lambda b: (0, 0)),
            pl.BlockSpec((len(POOL_WINDOWS), B_GROUP_DIM, B_GROUP_DIM), lambda b: (0, 0, 0)),
            pl.BlockSpec((len(POOL_WINDOWS), B_GROUP_DIM), lambda b: (0, 0)),
        ],
        out_specs=pl.BlockSpec((SEQ, B_WIDTH), lambda b: (b, 0)),
        out_shape=jax.ShapeDtypeStruct((T, B_WIDTH), BF),
        compiler_params=_cparams(("arbitrary",)),
        name="pool",
    )(y, hist2d, w_pool, pool_scale)


def _proj_res_body(*refs, split):
    refs = list(refs)
    parts = []
    for is_pair in split:
        parts.append((refs.pop(0), refs.pop(0) if is_pair else None))
    w_ref, o_ref = refs
    acts, (res_ref, res_tail_ref) = parts[:-1], parts[-1]
    w = w_ref[...].astype(BF)
    acc = res_ref[...]
    off = 0
    for a_ref, _ in acts:
        k = a_ref.shape[1]
        acc = acc + _bdot(a_ref[...], w[off:off + k, :])
        off += k
    o_ref[...] = acc

    if any(split):
        @pl.when(pl.program_id(1) == N_TM - 1)
        def _():
            tail = res_ref[TAIL_P:, :] if res_tail_ref is None else res_tail_ref[...]
            off = 0
            for a_ref, a_tail_ref in acts:
                k = a_ref.shape[1]
                a = a_ref[TAIL_P:, :] if a_tail_ref is None else a_tail_ref[...]
                tail = tail + _bdot(a, w[off:off + k, :])
                off += k
            o_ref[TAIL_P:, :] = tail


def _proj_res(acts, w, res, name):
    tn = 1024
    k_total = w.shape[0]
    operands, in_specs, split = [], [], []
    for n, x in enumerate(list(acts) + [res]):
        is_res = n == len(acts)
        pair = isinstance(x, tuple)
        split.append(pair)
        for part, rows in zip(x if pair else (x,), (TM, T_S)):
            cols = tn if is_res else part.shape[1]
            operands.append(part)
            if rows == TM:
                in_specs.append(pl.BlockSpec((TM, cols), (lambda j, i: (i, j)) if is_res
                                             else (lambda j, i: (i, 0))))
            else:
                in_specs.append(pl.BlockSpec((T_S, cols), (lambda j, i: (0, j)) if is_res
                                             else (lambda j, i: (0, 0))))
    return pl.pallas_call(
        functools.partial(_proj_res_body, split=tuple(split)),
        grid=(D_MODEL // tn, N_TM),
        in_specs=in_specs + [pl.BlockSpec((k_total, tn), lambda j, i: (0, j))],
        out_specs=pl.BlockSpec((TM, tn), lambda j, i: (i, j)),
        out_shape=jax.ShapeDtypeStruct((T, D_MODEL), F32),
        compiler_params=_cparams(("arbitrary", "arbitrary")),
        name=name,
    )(*operands, w)


def _ffn_body(x_hbm, g_ref, wg_ref, wu_ref, wd_ref, o_hbm, acc_ref, xn_ref, sem):
    i = pl.program_id(0)
    f = pl.program_id(1)
    rows = pl.ds(i * TM_FFN, TM_FFN)

    @pl.when(f == 0)
    def _():
        cp = pltpu.make_async_copy(x_hbm.at[rows], acc_ref, sem)
        cp.start()
        cp.wait()
        xn_ref[...] = _rms(acc_ref[...], g_ref[...]).astype(BF)

    xn = xn_ref[...]
    gate = _bdot(xn, wg_ref[...].astype(BF))
    up = _bdot(xn, wu_ref[...].astype(BF))
    act = (gate * jax.nn.sigmoid(gate) * up).astype(BF)
    acc_ref[...] += _bdot(act, wd_ref[...].astype(BF))

    @pl.when(f == pl.num_programs(1) - 1)
    def _():
        cp = pltpu.make_async_copy(acc_ref, o_hbm.at[rows], sem)
        cp.start()
        cp.wait()


def _ffn(x, g, wg, wu, wd):
    return pl.pallas_call(
        _ffn_body,
        grid=(T // TM_FFN, D_FF // TF_FFN),
        in_specs=[
            pl.BlockSpec(memory_space=pl.ANY),
            pl.BlockSpec((1, D_MODEL), lambda i, f: (0, 0)),
            pl.BlockSpec((D_MODEL, TF_FFN), lambda i, f: (0, f)),
            pl.BlockSpec((D_MODEL, TF_FFN), lambda i, f: (0, f)),
            pl.BlockSpec((TF_FFN, D_MODEL), lambda i, f: (f, 0)),
        ],
        out_specs=pl.BlockSpec(memory_space=pl.ANY),
        out_shape=jax.ShapeDtypeStruct((T, D_MODEL), F32),
        scratch_shapes=[pltpu.VMEM((TM_FFN, D_MODEL), F32),
                        pltpu.VMEM((TM_FFN, D_MODEL), BF),
                        pltpu.SemaphoreType.DMA(())],
        compiler_params=_cparams(("arbitrary", "arbitrary")),
        name="ffn",
    )(x, g, wg, wu, wd)


def _dqkv_body(x_ref, g_ref, w_ref, gq_ref, gkv_ref, cos_ref, sin_ref,
               cq_ref, ckv_ref, pe_ref, lat_ref, lat_s_ref):
    xn = _rms(x_ref[...], g_ref[...]).astype(BF)
    d = _bdot(xn, w_ref[...].astype(BF))
    cq_ref[...] = _rms(d[:, :Q_LORA], gq_ref[...]).astype(BF)
    ckv = _rms(d[:, Q_LORA:Q_LORA + KV_LORA], gkv_ref[...])
    pe = _rope128(d[:, Q_LORA + KV_LORA:], cos_ref[...], sin_ref[...])
    ckv_ref[...] = ckv.astype(BF)
    pe_ref[...] = pe.astype(BF)
    lat_ref[:, :KV_LORA] = ckv
    lat_ref[:, KV_LORA:] = pe[:, :QK_ROPE]

    @pl.when(pl.program_id(0) == N_TM - 1)
    def _():
        lat_s_ref[:, :KV_LORA] = ckv[TAIL_P:, :]
        lat_s_ref[:, KV_LORA:] = pe[TAIL_P:, :QK_ROPE]


def _dqkv(x, g, w_pad, gq, gkv, cos_t, sin_t):
    n = w_pad.shape[1]
    row = lambda i: (i, 0)
    fix = lambda i: (0, 0)
    return pl.pallas_call(
        _dqkv_body,
        grid=(N_TM,),
        in_specs=[
            pl.BlockSpec((TM, D_MODEL), row),
            pl.BlockSpec((1, D_MODEL), fix),
            pl.BlockSpec((D_MODEL, n), fix),
            pl.BlockSpec((1, Q_LORA), fix),
            pl.BlockSpec((1, KV_LORA), fix),
            pl.BlockSpec((TM, 128), row),
            pl.BlockSpec((TM, 128), row),
        ],
        out_specs=[
            pl.BlockSpec((TM, Q_LORA), row),
            pl.BlockSpec((TM, KV_LORA), row),
            pl.BlockSpec((TM, 128), row),
            pl.BlockSpec((TM, LATENT), row),
            pl.BlockSpec((T_S, LATENT), fix),
        ],
        out_shape=[
            jax.ShapeDtypeStruct((T, Q_LORA), BF),
            jax.ShapeDtypeStruct((T, KV_LORA), BF),
            jax.ShapeDtypeStruct((T, 128), BF),
            jax.ShapeDtypeStruct((T_P, LATENT), F32),
            jax.ShapeDtypeStruct((T_S, LATENT), F32),
        ],
        compiler_params=_cparams(("arbitrary",)),
        name="dqkv",
    )(x, g, w_pad, gq, gkv, cos_t, sin_t)


HG = 4


def _qkv_body(cq_ref, ckv_ref, pe_ref, cos_ref, sin_ref, wq_ref, wk_ref, wvt_ref,
              q_ref, k_ref, vt_ref):
    q = _bdot(cq_ref[...], wq_ref[...].astype(BF)) * np.float32(ATTN_SCALE * LOG2_E)
    ckv = ckv_ref[...]
    kn = _bdot(ckv, wk_ref[...].astype(BF))
    pe = pe_ref[...]
    c = cos_ref[...]
    s = sin_ref[...]
    for h in range(HG):
        o = h * HEAD_PAD
        q_ref[:, o:o + QK_NOPE] = q[:, o:o + QK_NOPE].astype(BF)
        q_ref[:, o + QK_NOPE:o + HEAD_PAD] = _rope128(q[:, o + QK_NOPE:o + HEAD_PAD], c, s).astype(BF)
        k_ref[:, o:o + QK_NOPE] = kn[:, h * QK_NOPE:(h + 1) * QK_NOPE].astype(BF)
        k_ref[:, o + QK_NOPE:o + HEAD_PAD] = pe
    vt_ref[...] = _bdot_nt(wvt_ref[...].astype(BF), ckv).astype(BF)


def _qkv(cq, ckv, pe, cos_t, sin_t, wq_pad, wk, wv_t):
    row = lambda j, i: (i, 0)
    col = lambda j, i: (0, j)
    out = lambda j, i: (i, j)
    return pl.pallas_call(
        _qkv_body,
        grid=(N_HEADS // HG, N_TM),
        in_specs=[
            pl.BlockSpec((TM, Q_LORA), row),
            pl.BlockSpec((TM, KV_LORA), row),
            pl.BlockSpec((TM, 128), row),
            pl.BlockSpec((TM, 128), row),
            pl.BlockSpec((TM, 128), row),
            pl.BlockSpec((Q_LORA, HG * HEAD_PAD), col),
            pl.BlockSpec((KV_LORA, HG * QK_NOPE), col),
            pl.BlockSpec((HG * V_HEAD, KV_LORA), lambda j, i: (j, 0)),
        ],
        out_specs=[
            pl.BlockSpec((TM, HG * HEAD_PAD), out),
            pl.BlockSpec((TM, HG * HEAD_PAD), out),
            pl.BlockSpec((HG * V_HEAD, TM), lambda j, i: (j, i)),
        ],
        out_shape=[
            jax.ShapeDtypeStruct((T, N_HEADS * HEAD_PAD), BF),
            jax.ShapeDtypeStruct((T, N_HEADS * HEAD_PAD), BF),
            jax.ShapeDtypeStruct((N_HEADS * V_HEAD, T), BF),
        ],
        compiler_params=_cparams(("arbitrary", "arbitrary")),
        name="qkv",
    )(cq, ckv, pe, cos_t, sin_t, wq_pad, wk, wv_t)


_PAIRS = [(qi, ki) for qi in range(N_QT) for ki in range(qi + 1)]


def _flash_body(qi_tab, ki_tab, q_ref, k_ref, vt_ref, o_ref, m_ref, l_ref, acc_ref):
    p = pl.program_id(1)
    qi = qi_tab[p]
    ki = ki_tab[p]

    @pl.when(ki == 0)
    def _():
        m_ref[...] = jnp.full(m_ref.shape, -jnp.inf, F32)
        l_ref[...] = jnp.zeros(l_ref.shape, F32)
        acc_ref[...] = jnp.zeros(acc_ref.shape, F32)

    def all_heads(on_diagonal):
        if on_diagonal:
            key = lax.broadcasted_iota(jnp.int32, (TQ, TQ), 0)
            qry = lax.broadcasted_iota(jnp.int32, (TQ, TQ), 1)
            visible = key <= qry
        for h in range(N_HEADS):
            qk = slice(h * HEAD_PAD, (h + 1) * HEAD_PAD)
            vs = slice(h * V_HEAD, (h + 1) * V_HEAD)
            st = _bdot_nt(k_ref[:, qk], q_ref[:, qk])
            if on_diagonal:
                st = jnp.where(visible, st, -jnp.inf)
            m_prev = m_ref[h]
            m_new = jnp.maximum(m_prev, jnp.max(st, axis=0, keepdims=True))
            alpha = jnp.exp2(m_prev - m_new)
            pt = jnp.exp2(st - m_new)
            l_ref[h] = alpha * l_ref[h] + jnp.sum(pt, axis=0, keepdims=True)
            acc_ref[vs, :] = alpha * acc_ref[vs, :] + _bdot(vt_ref[vs, :], pt.astype(BF))
            m_ref[h] = m_new

    @pl.when(ki < qi)
    def _():
        all_heads(False)

    @pl.when(ki == qi)
    def _():
        all_heads(True)
        for h in range(N_HEADS):
            vs = slice(h * V_HEAD, (h + 1) * V_HEAD)
            o_ref[:, vs] = (acc_ref[vs, :] / l_ref[h]).T.astype(BF)


def _flash(q, k, vt):
    qi_tab = jnp.asarray(np.array([p[0] for p in _PAIRS], np.int32))
    ki_tab = jnp.asarray(np.array([p[1] for p in _PAIRS], np.int32))
    grid_spec = pltpu.PrefetchScalarGridSpec(
        num_scalar_prefetch=2,
        grid=(BATCH, len(_PAIRS)),
        in_specs=[
            pl.BlockSpec((TQ, N_HEADS * HEAD_PAD), lambda b, p, qt, kt: (b * N_QT + qt[p], 0)),
            pl.BlockSpec((TQ, N_HEADS * HEAD_PAD), lambda b, p, qt, kt: (b * N_QT + kt[p], 0)),
            pl.BlockSpec((N_HEADS * V_HEAD, TQ), lambda b, p, qt, kt: (0, b * N_QT + kt[p])),
        ],
        out_specs=pl.BlockSpec((TQ, N_HEADS * V_HEAD), lambda b, p, qt, kt: (b * N_QT + qt[p], 0)),
        scratch_shapes=[
            pltpu.VMEM((N_HEADS, 1, TQ), F32),
            pltpu.VMEM((N_HEADS, 1, TQ), F32),
            pltpu.VMEM((N_HEADS * V_HEAD, TQ), F32),
        ],
    )
    return pl.pallas_call(
        _flash_body,
        grid_spec=grid_spec,
        out_shape=jax.ShapeDtypeStruct((T_P, N_HEADS * V_HEAD), BF),
        compiler_params=_cparams(("arbitrary", "arbitrary")),
        name="flash",
    )(qi_tab, ki_tab, q, k, vt)


def _qlat_body(q_ref, wk_ref, o_ref):
    o_ref[...] = _bdot_nt(q_ref[:, :QK_NOPE], wk_ref[...].astype(BF)).astype(BF)


def _qlat(q, wk):
    return pl.pallas_call(
        _qlat_body,
        grid=(N_HEADS,),
        in_specs=[
            pl.BlockSpec((T_S, HEAD_PAD), lambda h: (T_P // T_S, h)),
            pl.BlockSpec((KV_LORA, QK_NOPE), lambda h: (0, h)),
        ],
        out_specs=pl.BlockSpec((T_S, KV_LORA), lambda h: (0, h)),
        out_shape=jax.ShapeDtypeStruct((T_S, N_HEADS * KV_LORA), BF),
        compiler_params=_cparams(("arbitrary",)),
        name="qlat",
    )(q, wk)


def _decode_body(pt_ref, qlat_ref, q_ref, lat_ref, cache_hbm, o_ref,
                 buf, kb_ref, sems, m_ref, l_ref, acc_ref):
    b = pl.program_id(0)
    c = pl.program_id(1)
    step = b * N_KV_STEPS + c
    n_steps = DEC_BATCH * N_KV_STEPS
    slot = step % KV_SLOTS

    def page_copy(st, j, sl):
        pg = pt_ref[st * PAGES_PER_STEP + j]
        return pltpu.make_async_copy(cache_hbm.at[pg], buf.at[sl, j], sems.at[sl])

    def start_fetch(st, sl):
        for j in range(PAGES_PER_STEP):
            page_copy(st, j, sl).start()

    @pl.when(step == 0)
    def _():
        for s0 in range(KV_SLOTS - 1):
            start_fetch(s0, s0)

    ahead = step + (KV_SLOTS - 1)

    @pl.when(ahead < n_steps)
    def _():
        start_fetch(ahead, ahead % KV_SLOTS)

    qlat = qlat_ref[0]
    qpe = q_ref[0, :, QK_NOPE:QK_NOPE + QK_ROPE]

    @pl.when(c == 0)
    def _():
        lat = lat_ref[pl.ds(b, 1), :]
        s_self = (jnp.sum(qlat.astype(F32) * lat[:, :KV_LORA], axis=-1, keepdims=True)
                  + jnp.sum(qpe.astype(F32) * lat[:, KV_LORA:], axis=-1, keepdims=True))
        m_ref[...] = s_self
        l_ref[...] = jnp.ones(l_ref.shape, F32)
        acc_ref[...] = jnp.broadcast_to(lat[:, :KV_LORA], acc_ref.shape)

    for j in range(PAGES_PER_STEP):
        page_copy(step, j, slot).wait()

    s_parts = []
    for j in range(PAGES_PER_STEP):
        kb_ref[j] = buf[slot, j].astype(BF)
        s_parts.append(_bdot(qlat, kb_ref[j, :KV_LORA, :]) + _bdot(qpe, kb_ref[j, KV_LORA:, :]))
    s = jnp.concatenate(s_parts, axis=1)
    m_prev = m_ref[...]
    m_new = jnp.maximum(m_prev, jnp.max(s, axis=-1, keepdims=True))
    alpha = jnp.exp2(m_prev - m_new)
    pexp = jnp.exp2(s - m_new)
    l_ref[...] = alpha * l_ref[...] + jnp.sum(pexp, axis=-1, keepdims=True)
    pb = pexp.astype(BF)
    pv = jnp.zeros(acc_ref.shape, F32)
    for j in range(PAGES_PER_STEP):
        pv = pv + _bdot_nt(pb[:, j * PAGE_SIZE:(j + 1) * PAGE_SIZE], kb_ref[j, :KV_LORA, :])
    acc_ref[...] = alpha * acc_ref[...] + pv
    m_ref[...] = m_new

    @pl.when(c == N_KV_STEPS - 1)
    def _():
        o_ref[0] = acc_ref[...] / l_ref[...]


def _decode(page_table_flat, qlat3, q3, lat_s, cache_t):
    grid_spec = pltpu.PrefetchScalarGridSpec(
        num_scalar_prefetch=1,
        grid=(DEC_BATCH, N_KV_STEPS),
        in_specs=[
            pl.BlockSpec((1, N_HEADS, KV_LORA), lambda b, c, pt: (b, 0, 0)),
            pl.BlockSpec((1, N_HEADS, HEAD_PAD), lambda b, c, pt: (b, 0, 0)),
            pl.BlockSpec((T_S, LATENT), lambda b, c, pt: (0, 0)),
            pl.BlockSpec(memory_space=pl.ANY),
        ],
        out_specs=pl.BlockSpec((1, N_HEADS, KV_LORA), lambda b, c, pt: (b, 0, 0)),
        scratch_shapes=[
            pltpu.VMEM((KV_SLOTS, PAGES_PER_STEP, LATENT, PAGE_SIZE), F32),
            pltpu.VMEM((PAGES_PER_STEP, LATENT, PAGE_SIZE), BF),
            pltpu.SemaphoreType.DMA((KV_SLOTS,)),
            pltpu.VMEM((N_HEADS, 1), F32),
            pltpu.VMEM((N_HEADS, 1), F32),
            pltpu.VMEM((N_HEADS, KV_LORA), F32),
        ],
    )
    return pl.pallas_call(
        _decode_body,
        grid_spec=grid_spec,
        out_shape=jax.ShapeDtypeStruct((DEC_BATCH, N_HEADS, KV_LORA), F32),
        compiler_params=_cparams(("arbitrary", "arbitrary")),
        name="decode",
    )(page_table_flat, qlat3, q3, lat_s, cache_t)


def _ouv_body(o_ref, wv_ref, out_ref):
    out_ref[...] = _bdot(o_ref[...].astype(BF), wv_ref[...].astype(BF)).astype(BF)


def _ouv(olat2d, wv):
    return pl.pallas_call(
        _ouv_body,
        grid=(N_HEADS,),
        in_specs=[
            pl.BlockSpec((T_S, KV_LORA), lambda h: (0, h)),
            pl.BlockSpec((KV_LORA, V_HEAD), lambda h: (0, h)),
        ],
        out_specs=pl.BlockSpec((T_S, V_HEAD), lambda h: (0, h)),
        out_shape=jax.ShapeDtypeStruct((T_S, N_HEADS * V_HEAD), BF),
        compiler_params=_cparams(("arbitrary",)),
        name="ouv",
    )(olat2d, wv)


def _router_body(x_ref, g_ref, w_ref, meta_ref):
    xn = _rms(x_ref[...], g_ref[...])
    w = w_ref[...]
    xh = xn.astype(BF)
    xl = (xn - xh.astype(F32)).astype(BF)
    wh = w.astype(BF)
    wl = (w - wh.astype(F32)).astype(BF)
    logits = _bdot(xh, wh) + (_bdot(xh, wl) + _bdot(xl, wh))
    lane = lax.broadcasted_iota(jnp.int32, logits.shape, 1).astype(F32)
    lg = jnp.where(lane < N_EXPERTS, logits, -jnp.inf)
    t1 = jnp.max(lg, axis=-1, keepdims=True)
    i1 = jnp.min(jnp.where(lg == t1, lane, 128.0), axis=-1, keepdims=True)
    lg2 = jnp.where(lane == i1, -jnp.inf, lg)
    t2 = jnp.max(lg2, axis=-1, keepdims=True)
    i2 = jnp.min(jnp.where(lg2 == t2, lane, 128.0), axis=-1, keepdims=True)
    e = jnp.exp(t2 - t1)
    g1 = 1.0 / (1.0 + e)
    g2 = e / (1.0 + e)
    meta = jnp.where(lane == 0, i1, jnp.where(lane == 1, i2, jnp.where(lane == 2, g1, g2)))
    meta_ref[...] = meta


def _router(x, g, w_pad):
    return pl.pallas_call(
        _router_body,
        grid=(N_TM,),
        in_specs=[
            pl.BlockSpec((TM, D_MODEL), lambda i: (i, 0)),
            pl.BlockSpec((1, D_MODEL), lambda i: (0, 0)),
            pl.BlockSpec((D_MODEL, 128), lambda i: (0, 0)),
        ],
        out_specs=pl.BlockSpec((TM, 128), lambda i: (i, 0)),
        out_shape=jax.ShapeDtypeStruct((T, 128), F32),
        compiler_params=_cparams(("arbitrary",)),
        name="router",
    )(x, g, w_pad)


N_FF = D_FF // TF
N_CH = MOE_TM // MOE_CH


def _moe_body(te_ref, rows_ref, ybase_ref, src_ref, h_hbm, g_ref, wg_ref, wu_ref, wd_ref, y_hbm,
              acc_ref, xn_ref, wgb_ref, wub_ref, wdb_ref, zero_ref, gsems, wsem):
    t = pl.program_id(0)
    f = pl.program_id(1)
    nch = (rows_ref[t] + MOE_CH - 1) // MOE_CH
    big = MOE_BLOCKS[0]
    nbig = nch // big
    base = pl.multiple_of(ybase_ref[t], MOE_CH)

    def remainders():
        out = []
        ch0 = nbig * big
        left = nch - ch0
        for size in MOE_BLOCKS[1:]:
            present = left >= size
            out.append((ch0, size, present))
            ch0 = ch0 + jnp.where(present, size, 0)
            left = left - jnp.where(present, size, 0)
        return out

    def issue_gather(ch):
        def body(r0, carry):
            for j in range(DMA_UNROLL):
                r = ch * MOE_CH + r0 * DMA_UNROLL + j
                pltpu.make_async_copy(h_hbm.at[pl.ds(src_ref[base + r], 1)],
                                      acc_ref.at[pl.ds(r, 1)], gsems.at[ch]).start()
            return carry
        lax.fori_loop(0, MOE_CH // DMA_UNROLL, body, 0)

    def wait_gather(ch):
        pltpu.make_async_copy(h_hbm.at[pl.ds(0, MOE_CH)],
                              acc_ref.at[pl.ds(ch * MOE_CH, MOE_CH)], gsems.at[ch]).wait()

    def writeback(row0, nr):
        return pltpu.make_async_copy(acc_ref.at[pl.ds(row0, nr)],
                                     y_hbm.at[pl.ds(pl.multiple_of(base + row0, MOE_CH), nr)], wsem)

    def cast_weights():
        wg = wg_ref[0].astype(BF)
        wu = wu_ref[0].astype(BF)
        wd = wd_ref[0].astype(BF)
        wgb_ref[...] = wg
        wub_ref[...] = wu
        wdb_ref[...] = wd
        return wg, wu, wd

    def block(ch0, nr, first, last, cast=False):
        row0 = pl.multiple_of(ch0 * MOE_CH, MOE_CH)
        rs = pl.ds(row0, nr)
        if first:
            for c in range(nr // MOE_CH):
                wait_gather(ch0 + c)
            xn_ref[rs, :] = _rms(acc_ref[rs, :], g_ref[...]).astype(BF)
        xn = xn_ref[rs, :]
        wg, wu, wd = cast_weights() if cast else (wgb_ref[...], wub_ref[...], wdb_ref[...])
        gate = _bdot(xn, wg)
        up = _bdot(xn, wu)
        act = (gate * jax.nn.sigmoid(gate) * up).astype(BF)
        out = _bdot(act, wd)
        if first:
            acc_ref[rs, :] = out
        else:
            acc_ref[rs, :] += out
        if last:
            writeback(row0, nr).start()

    def sweep(first, last):
        if first:
            lax.fori_loop(0, nch, lambda ch, c: (issue_gather(ch), c)[1], 0)
        @pl.when(nbig > 0)
        def _():
            block(0, big * MOE_CH, first, last, cast=True)

        @pl.when(nbig == 0)
        def _():
            cast_weights()

        lax.fori_loop(1, nbig, lambda b, c: (block(big * b, big * MOE_CH, first, last), c)[1], 0)
        rems = remainders()
        for ch0, size, present in rems:
            @pl.when(present)
            def _():
                block(ch0, size * MOE_CH, first, last)

        if last:
            lax.fori_loop(0, nbig, lambda b, c: (writeback(0, big * MOE_CH).wait(), c)[1], 0)
            for ch0, size, present in rems:
                @pl.when(present)
                def _():
                    writeback(0, size * MOE_CH).wait()

    @pl.when(nch > 0)
    def _():
        @pl.when(f == 0)
        def _():
            sweep(True, False)

        @pl.when(jnp.logical_and(f > 0, f < N_FF - 1))
        def _():
            sweep(False, False)

        @pl.when(f == N_FF - 1)
        def _():
            sweep(False, True)

    @pl.when(jnp.logical_and(t == MOE_NT - 1, f == N_FF - 1))
    def _():
        used = (base + nch * MOE_CH) // MOE_CH
        zero_ref[...] = jnp.zeros(zero_ref.shape, F32)

        def fill(c):
            return pltpu.make_async_copy(
                zero_ref, y_hbm.at[pl.ds(pl.multiple_of(c * MOE_CH, MOE_CH), MOE_CH)], wsem)

        lax.fori_loop(used, MOE_ROWS // MOE_CH, lambda c, k: (fill(c).start(), k)[1], 0)
        lax.fori_loop(used, MOE_ROWS // MOE_CH, lambda c, k: (fill(0).wait(), k)[1], 0)


def _moe(tile_expert, tile_rows, ybase, src, h, g, wg, wu, wd):
    ff = lambda t, f, nr: jnp.where(nr[t] > 0, f, N_FF - 1)
    grid_spec = pltpu.PrefetchScalarGridSpec(
        num_scalar_prefetch=4,
        grid=(MOE_NT, N_FF),
        in_specs=[
            pl.BlockSpec(memory_space=pl.ANY),
            pl.BlockSpec((1, D_MODEL), lambda t, f, te, nr, yb, sr: (0, 0)),
            pl.BlockSpec((1, D_MODEL, TF), lambda t, f, te, nr, yb, sr: (te[t], 0, ff(t, f, nr))),
            pl.BlockSpec((1, D_MODEL, TF), lambda t, f, te, nr, yb, sr: (te[t], 0, ff(t, f, nr))),
            pl.BlockSpec((1, TF, D_MODEL), lambda t, f, te, nr, yb, sr: (te[t], ff(t, f, nr), 0)),
        ],
        out_specs=pl.BlockSpec(memory_space=pl.ANY),
        scratch_shapes=[
            pltpu.VMEM((MOE_TM, D_MODEL), F32),
            pltpu.VMEM((MOE_TM, D_MODEL), BF),
            pltpu.VMEM((D_MODEL, TF), BF),
            pltpu.VMEM((D_MODEL, TF), BF),
            pltpu.VMEM((TF, D_MODEL), BF),
            pltpu.VMEM((MOE_CH, D_MODEL), F32),
            pltpu.SemaphoreType.DMA((N_CH,)),
            pltpu.SemaphoreType.DMA(()),
        ],
    )
    return pl.pallas_call(
        _moe_body,
        grid_spec=grid_spec,
        out_shape=jax.ShapeDtypeStruct((MOE_ROWS, D_MODEL), F32),
        compiler_params=_cparams(("arbitrary", "arbitrary")),
        name="moe",
    )(tile_expert, tile_rows, ybase, src, h, g, wg, wu, wd)


def _combine_body(pos_ref, h_ref, meta_ref, gf_ref, y_hbm, yp_ref, ys_ref, yb_ref, sems):
    i = pl.program_id(0)
    slot = i % 2

    def fetch(tile, sl):
        rows_per_iter = DMA_UNROLL // TOP_K

        def issue(r0, carry):
            for j in range(rows_per_iter):
                r = r0 * rows_per_iter + j
                for k in range(TOP_K):
                    pltpu.make_async_copy(y_hbm.at[pl.ds(pos_ref[TOP_K * (tile * TC + r) + k], 1)],
                                          yb_ref.at[sl, k, pl.ds(r, 1)], sems.at[sl]).start()
            return carry
        lax.fori_loop(0, TC // rows_per_iter, issue, 0)

    @pl.when(i == 0)
    def _():
        fetch(0, 0)

    @pl.when(i + 1 < N_TC)
    def _():
        fetch(i + 1, 1 - slot)

    for k in range(TOP_K):
        pltpu.make_async_copy(y_hbm.at[pl.ds(0, TC)], yb_ref.at[slot, k], sems.at[slot]).wait()
    meta = meta_ref[...]
    out = h_ref[...] + meta[:, 2:3] * yb_ref[slot, 0] + meta[:, 3:4] * yb_ref[slot, 1]
    out = _rms(out, gf_ref[...])

    @pl.when(i < N_TC_P)
    def _():
        yp_ref[...] = out

    @pl.when(i >= N_TC_P)
    def _():
        ys_ref[...] = out


def _combine(pos, h, meta, gf, y):
    grid_spec = pltpu.PrefetchScalarGridSpec(
        num_scalar_prefetch=1,
        grid=(N_TC,),
        in_specs=[
            pl.BlockSpec((TC, D_MODEL), lambda i, pos: (i, 0)),
            pl.BlockSpec((TC, 128), lambda i, pos: (i, 0)),
            pl.BlockSpec((1, D_MODEL), lambda i, pos: (0, 0)),
            pl.BlockSpec(memory_space=pl.ANY),
        ],
        out_specs=[
            pl.BlockSpec((TC, D_MODEL), lambda i, pos: (jnp.minimum(i, N_TC_P - 1), 0)),
            pl.BlockSpec((T_S, D_MODEL), lambda i, pos: (0, 0)),
        ],
        scratch_shapes=[
            pltpu.VMEM((2, TOP_K, TC, D_MODEL), F32),
            pltpu.SemaphoreType.DMA((2,)),
        ],
    )
    return pl.pallas_call(
        _combine_body,
        grid_spec=grid_spec,
        out_shape=[
            jax.ShapeDtypeStruct((T_P, D_MODEL), F32),
            jax.ShapeDtypeStruct((T_S, D_MODEL), F32),
        ],
        compiler_params=_cparams(("arbitrary",)),
        name="combine",
    )(pos, h, meta, gf, y)


def _routing_tables(experts):
    flat = experts.reshape(-1)
    onehot = (flat[:, None] == jnp.arange(N_EXPERTS, dtype=jnp.int32)[None, :]).astype(jnp.int32)
    csum = jnp.cumsum(onehot, axis=0)
    rank = jnp.take_along_axis(csum, flat[:, None], axis=1)[:, 0] - 1
    counts = csum[-1]
    tiles_per = (counts + MOE_TM - 1) // MOE_TM
    tile_end = jnp.cumsum(tiles_per)
    tile_start = tile_end - tiles_per
    n_used = tile_end[-1]
    tid = jnp.arange(MOE_NT, dtype=jnp.int32)
    last_tile = jnp.maximum(n_used - 1, 0)
    te = jnp.sum(tile_end[None, :] <= jnp.minimum(tid, last_tile)[:, None], axis=1).astype(jnp.int32)
    te = jnp.minimum(te, N_EXPERTS - 1)
    rows = jnp.clip(counts[te] - (tid - tile_start[te]) * MOE_TM, 0, MOE_TM)
    rows = jnp.where(tid < n_used, rows, 0).astype(jnp.int32)
    rows_pad = (rows + MOE_CH - 1) // MOE_CH * MOE_CH
    ybase = (jnp.cumsum(rows_pad) - rows_pad).astype(jnp.int32)
    pos = (ybase[tile_start[flat] + rank // MOE_TM] + rank % MOE_TM).astype(jnp.int32)
    return te, rows, ybase, _gather_sources(pos), pos


def _gather_sources_body(pos_ref, src_ref):
    per_iter = DMA_UNROLL

    def clear(i, carry):
        for j in range(per_iter):
            src_ref[i * per_iter + j] = jnp.int32(0)
        return carry

    def put(i, carry):
        for j in range(per_iter):
            src_ref[pos_ref[i * per_iter + j]] = i * (per_iter // TOP_K) + j // TOP_K
        return carry

    lax.fori_loop(0, MOE_ROWS // per_iter, clear, 0)
    lax.fori_loop(0, (T * TOP_K) // per_iter, put, 0)


def _gather_sources(pos):
    assert MOE_ROWS % DMA_UNROLL == 0 and (T * TOP_K) % DMA_UNROLL == 0 and DMA_UNROLL % TOP_K == 0
    return pl.pallas_call(
        _gather_sources_body,
        in_specs=[pl.BlockSpec(memory_space=pltpu.SMEM)],
        out_specs=pl.BlockSpec(memory_space=pltpu.SMEM),
        out_shape=jax.ShapeDtypeStruct((MOE_ROWS,), jnp.int32),
        name="gather_sources",
    )(pos)


def _rope_tables():
    half = QK_ROPE // 2
    inv = 1.0 / (ROPE_THETA ** (jnp.arange(half, dtype=F32) * 2.0 / QK_ROPE))

    def tables(pos):
        ang = pos.astype(F32)[:, None] * inv[None, :]
        cos, sin = jnp.cos(ang), jnp.sin(ang)
        zero = jnp.zeros((pos.shape[0], 128 - QK_ROPE), F32)
        return (jnp.concatenate([cos, cos, zero], axis=1),
                jnp.concatenate([-sin, sin, zero], axis=1))

    cos_p, sin_p = tables(jnp.arange(SEQ))
    cos_s, sin_s = tables(jnp.full((1,), PAST_LEN))
    stack = lambda p, s: jnp.concatenate([jnp.tile(p, (BATCH, 1)), jnp.broadcast_to(s, (T_S, 128))])
    return stack(cos_p, cos_s), stack(sin_p, sin_s)


def kernel(x_prompt, x_sample, state_pool, cache_mla, page_table, norm_mix, norm_ffn, w_in, g_v, w_s, b_s, w_pool, pool_scale, w_o_mix, w_ffn_gate, w_ffn_up, w_ffn_down, w_dqkv, g_q, g_kv, w_uq, w_uk, w_uv, w_o_attn, w_router, w_exp_gate, w_exp_up, w_exp_down, norm_final):
    x_p = x_prompt.reshape(T_P, D_MODEL)
    x_s = x_sample.reshape(T_S, D_MODEL)

    y = _inproj(x_p, x_s, norm_mix[0:1], w_in[0], g_v)
    b_bcast = jnp.broadcast_to(b_s[0][:, :, None], (A_HEADS, CHUNK, CHUNK))
    a_out = _gate(y, w_s[0], b_bcast)
    hist2d = state_pool[0].reshape(T_S, POOL_HIST * B_WIDTH)
    b_out = _pool(y, hist2d, w_pool[0], pool_scale[0])
    h = _proj_res([a_out, b_out], w_o_mix[0], (x_p, x_s), "mix_out")
    h = _ffn(h, norm_ffn[0:1], w_ffn_gate[0], w_ffn_up[0], w_ffn_down[0])

    pool_state_prompt = jnp.stack(
        [y[2, (b + 1) * SEQ - POOL_HIST:(b + 1) * SEQ] for b in range(BATCH)])[None]
    z_s = y[2, T_P:]
    pool_state_sample = jnp.concatenate([state_pool[0][:, 1:], z_s[:, None, :]], axis=1)[None]
    chunk_v_sample = y[1, T_P:].reshape(1, T_S, 1, A_WIDTH)

    cos_t, sin_t = _rope_tables()
    w_dqkv_pad = jnp.pad(w_dqkv[0], ((0, 0), (0, 128 - QK_ROPE)))
    cq, ckv, pe, lat_p, lat_s = _dqkv(h, norm_mix[1:2], w_dqkv_pad, g_q, g_kv, cos_t, sin_t)
    wq_pad = jnp.pad(w_uq[0].reshape(Q_LORA, N_HEADS, QK_NOPE + QK_ROPE),
                     ((0, 0), (0, 0), (0, HEAD_PAD - QK_NOPE - QK_ROPE))).reshape(Q_LORA, N_HEADS * HEAD_PAD)
    wk2d = w_uk[0].reshape(KV_LORA, N_HEADS * QK_NOPE)
    wv2d = w_uv[0].reshape(KV_LORA, N_HEADS * V_HEAD)
    q, k, vt = _qkv(cq, ckv, pe, cos_t, sin_t, wq_pad, wk2d, wv2d.T)
    attn = _flash(q, k, vt)

    qlat = _qlat(q, wk2d)
    cache_t = jnp.swapaxes(cache_mla, 2, 3).reshape(-1, LATENT, PAGE_SIZE)
    olat = _decode(page_table.reshape(-1),
                   qlat.reshape(T_S, N_HEADS, KV_LORA),
                   q[T_P:].reshape(T_S, N_HEADS, HEAD_PAD),
                   lat_s,
                   cache_t)
    attn_s = _ouv(olat.reshape(T_S, N_HEADS * KV_LORA), wv2d)
    h = _proj_res([(attn, attn_s)], w_o_attn[0], h, "attn_out")

    mla_rows_prompt = lat_p.reshape(1, BATCH, SEQ, LATENT)
    mla_rows_sample = lat_s.reshape(1, T_S, 1, LATENT)

    w_router_pad = jnp.pad(w_router[0], ((0, 0), (0, 128 - N_EXPERTS)))
    meta = _router(h, norm_ffn[1:2], w_router_pad)
    experts = meta[:, :TOP_K].astype(jnp.int32)
    te, rows, ybase, src, pos = _routing_tables(experts)
    y_sorted = _moe(te, rows, ybase, src, h, norm_ffn[1:2],
                    w_exp_gate[0], w_exp_up[0], w_exp_down[0])
    y_p, y_s = _combine(pos, h, meta, norm_final.reshape(1, D_MODEL), y_sorted)

    return (y_p.reshape(BATCH, SEQ, D_MODEL), y_s.reshape(T_S, 1, D_MODEL),
            pool_state_prompt, pool_state_sample, chunk_v_sample,
            mla_rows_prompt, mla_rows_sample)
```

```python
import functools

import numpy as np
import jax
import jax.numpy as jnp
from jax import lax
from jax.experimental import pallas as pl
from jax.experimental.pallas import tpu as pltpu

D_MODEL = 2048
BATCH = 4
SEQ = 2048
DEC_BATCH = 128
PAST_LEN = 8192
PAGE_SIZE = 128
N_PAGES = PAST_LEN // PAGE_SIZE
A_WIDTH = 1024
A_HEADS = 8
A_HEAD_DIM = 128
CHUNK = 128
B_WIDTH = 1024
POOL_WINDOWS = (2, 4, 8, 16)
B_GROUP_DIM = 256
POOL_HIST = 15
N_HEADS = 16
QK_NOPE = 128
QK_ROPE = 64
V_HEAD = 128
Q_LORA = 512
KV_LORA = 512
LATENT = KV_LORA + QK_ROPE
ROPE_THETA = 10000.0
ATTN_SCALE = (QK_NOPE + QK_ROPE) ** -0.5
LOG2_E = float(np.log2(np.e))
D_FF = 7168
N_EXPERTS = 8
TOP_K = 2
EPS = 1e-6

T_P = BATCH * SEQ
T_S = DEC_BATCH
T = T_P + T_S

BF = jnp.bfloat16
F32 = jnp.float32

VMEM_LIMIT_BYTES = 56 * 1024 * 1024

TM = 640
N_TM = T // TM
TAIL_P = T_P - (N_TM - 1) * TM
TM_FFN = 1040
TF = 256
TF_FFN = 512
HEAD_PAD = 256
TQ = 512
N_QT = SEQ // TQ
PAGES_PER_STEP = 32
N_KV_STEPS = N_PAGES // PAGES_PER_STEP
KV_SLOTS = 3
MOE_CH = 128
MOE_BLOCKS = (4, 2, 1)
MOE_TM = 18 * MOE_CH
DMA_UNROLL = 16
MOE_NT = (T * TOP_K) // MOE_TM + N_EXPERTS
MOE_ROWS = -(-(T * TOP_K + MOE_NT * (MOE_CH - 1)) // MOE_CH) * MOE_CH
TC = 128
N_TC = T // TC
N_TC_P = T_P // TC


def _cparams(sem):
    return pltpu.CompilerParams(dimension_semantics=sem, vmem_limit_bytes=VMEM_LIMIT_BYTES)


def _rms(x, g):
    return x * lax.rsqrt(jnp.mean(x * x, axis=-1, keepdims=True) + EPS) * g


def _gelu(x):
    return 0.5 * x * (1.0 + lax.erf(x * np.float32(np.sqrt(0.5))))


def _bdot(a, b):
    return jnp.dot(a, b, preferred_element_type=F32)


def _bdot_nt(a, b):
    return lax.dot_general(a, b, (((1,), (1,)), ((), ())), preferred_element_type=F32)


def _rope128(x, c, s):
    swapped = pltpu.roll(x, 96, axis=1) + pltpu.roll(x, 32, axis=1)
    return x * c + swapped * s


def _inproj_body(xp_ref, xs_ref, g_ref, w_ref, gv_ref, o_ref):
    j = pl.program_id(0)
    w = w_ref[...].astype(BF)

    def project(x_rows, out_rows):
        y = _bdot(_rms(x_rows, g_ref[...]).astype(BF), w)

        @pl.when(j == 0)
        def _():
            o_ref[0, out_rows, :] = _gelu(y)

        @pl.when(j == 1)
        def _():
            o_ref[0, out_rows, :] = _rms(_gelu(y), gv_ref[...])

        @pl.when(j == 2)
        def _():
            o_ref[0, out_rows, :] = y

    project(xp_ref[...], slice(None))

    @pl.when(pl.program_id(1) == N_TM - 1)
    def _():
        project(xs_ref[...], slice(TAIL_P, TM))


def _inproj(x_p, x_s, g, w, gv):
    return pl.pallas_call(
        _inproj_body,
        grid=(3, N_TM),
        in_specs=[
            pl.BlockSpec((TM, D_MODEL), lambda j, i: (i, 0)),
            pl.BlockSpec((T_S, D_MODEL), lambda j, i: (0, 0)),
            pl.BlockSpec((1, D_MODEL), lambda j, i: (0, 0)),
            pl.BlockSpec((D_MODEL, A_WIDTH), lambda j, i: (0, j)),
            pl.BlockSpec((1, A_WIDTH), lambda j, i: (0, 0)),
        ],
        out_specs=pl.BlockSpec((1, TM, A_WIDTH), lambda j, i: (j, i, 0)),
        out_shape=jax.ShapeDtypeStruct((3, T, A_WIDTH), F32),
        compiler_params=_cparams(("arbitrary", "arbitrary")),
        name="inproj",
    )(x_p, x_s, g, w, gv)


def _gate_body(u_ref, v_ref, ws_ref, bb_ref, a_ref):
    assert TAIL_P % CHUNK == 0 and T_S == CHUNK
    n_chunks = TM // CHUNK
    last_tile = pl.program_id(0) == N_TM - 1
    row = lax.broadcasted_iota(jnp.int32, (CHUNK, CHUNK), 0)
    col = lax.broadcasted_iota(jnp.int32, (CHUNK, CHUNK), 1)
    for h in range(A_HEADS):
        w = ws_ref[h]
        b = bb_ref[h]
        w_chunk = jnp.where(col <= row, w, 0.0)
        w_single = jnp.where(col == row, jnp.broadcast_to(w[0:1, 0:1], (CHUNK, CHUNK)), 0.0)
        w_prompt = w_chunk.astype(BF)
        w_tail = jnp.where(last_tile, w_single, w_chunk).astype(BF)
        b_tail = jnp.where(last_tile, jnp.broadcast_to(b[0:1, :], (CHUNK, CHUNK)), b)
        sl = slice(h * A_HEAD_DIM, (h + 1) * A_HEAD_DIM)
        for c in range(n_chunks):
            rows = slice(c * CHUNK, (c + 1) * CHUNK)
            tail = c == n_chunks - 1
            mixed = _bdot(w_tail if tail else w_prompt, v_ref[0, rows, sl].astype(BF))
            mixed = mixed + (b_tail if tail else b)
            a_ref[rows, sl] = (u_ref[0, rows, sl] * mixed).astype(BF)


def _gate(y, w_s, b_bcast):
    return pl.pallas_call(
        _gate_body,
        grid=(N_TM,),
        in_specs=[
            pl.BlockSpec((1, TM, A_WIDTH), lambda i: (0, i, 0)),
            pl.BlockSpec((1, TM, A_WIDTH), lambda i: (1, i, 0)),
            pl.BlockSpec((A_HEADS, CHUNK, CHUNK), lambda i: (0, 0, 0)),
            pl.BlockSpec((A_HEADS, CHUNK, CHUNK), lambda i: (0, 0, 0)),
        ],
        out_specs=pl.BlockSpec((TM, A_WIDTH), lambda i: (i, 0)),
        out_shape=jax.ShapeDtypeStruct((T, A_WIDTH), BF),
        compiler_params=_cparams(("arbitrary",)),
        name="gate",
    )(y, y, w_s, b_bcast)


def _pool_body(z_ref, hist_ref, wp_ref, ps_ref, o_ref):
    b = pl.program_id(0)

    def project(pooled, g):
        out = _bdot(pooled.astype(BF), wp_ref[g].astype(BF)) * ps_ref[g:g + 1, :]
        return out.astype(BF)

    @pl.when(b < BATCH)
    def _():
        row = lax.broadcasted_iota(jnp.int32, (SEQ, B_GROUP_DIM), 0)
        for g, w in enumerate(POOL_WINDOWS):
            sl = slice(g * B_GROUP_DIM, (g + 1) * B_GROUP_DIM)
            z = z_ref[0, :, sl]
            s = z
            k = 1
            while k < w:
                s = s + jnp.where(row >= k, pltpu.roll(s, k, axis=0), 0.0)
                k *= 2
            cnt = jnp.minimum(row + 1, w).astype(F32)
            o_ref[:, sl] = project(s / cnt - z, g)

    @pl.when(b == BATCH)
    def _():
        for g, w in enumerate(POOL_WINDOWS):
            sl = slice(g * B_GROUP_DIM, (g + 1) * B_GROUP_DIM)
            z = z_ref[0, 0:T_S, sl]
            s = z
            for k in range(POOL_HIST + 1 - w, POOL_HIST):
                s = s + hist_ref[:, k * B_WIDTH + g * B_GROUP_DIM:k * B_WIDTH + (g + 1) * B_GROUP_DIM]
            o_ref[0:T_S, sl] = project(s / np.float32(w) - z, g)


def _pool(y, hist2d, w_pool, pool_scale):
    return pl.pallas_call(
        _pool_body,
        grid=(BATCH + 1,),
        in_specs=[
            pl.BlockSpec((1, SEQ, B_WIDTH), lambda b: (2, b, 0)),
            pl.BlockSpec((T_S, POOL_HIST * B_WIDTH), lambda b: (0, 0)),
            pl.BlockSpec((len(POOL_WINDOWS), B_GROUP_DIM, B_GROUP_DIM), lambda b: (0, 0, 0)),
            pl.BlockSpec((len(POOL_WINDOWS), B_GROUP_DIM), lambda b: (0, 0)),
        ],
        out_specs=pl.BlockSpec((SEQ, B_WIDTH), lambda b: (b, 0)),
        out_shape=jax.ShapeDtypeStruct((T, B_WIDTH), BF),
        compiler_params=_cparams(("arbitrary",)),
        name="pool",
    )(y, hist2d, w_pool, pool_scale)


def _proj_res_body(*refs, split):
    refs = list(refs)
    parts = []
    for is_pair in split:
        parts.append((refs.pop(0), refs.pop(0) if is_pair else None))
    w_ref, o_ref = refs
    acts, (res_ref, res_tail_ref) = parts[:-1], parts[-1]
    w = w_ref[...].astype(BF)
    acc = res_ref[...]
    off = 0
    for a_ref, _ in acts:
        k = a_ref.shape[1]
        acc = acc + _bdot(a_ref[...], w[off:off + k, :])
        off += k
    o_ref[...] = acc

    if any(split):
        @pl.when(pl.program_id(1) == N_TM - 1)
        def _():
            tail = res_ref[TAIL_P:, :] if res_tail_ref is None else res_tail_ref[...]
            off = 0
            for a_ref, a_tail_ref in acts:
                k = a_ref.shape[1]
                a = a_ref[TAIL_P:, :] if a_tail_ref is None else a_tail_ref[...]
                tail = tail + _bdot(a, w[off:off + k, :])
                off += k
            o_ref[TAIL_P:, :] = tail


def _proj_res(acts, w, res, name):
    tn = 1024
    k_total = w.shape[0]
    operands, in_specs, split = [], [], []
    for n, x in enumerate(list(acts) + [res]):
        is_res = n == len(acts)
        pair = isinstance(x, tuple)
        split.append(pair)
        for part, rows in zip(x if pair else (x,), (TM, T_S)):
            cols = tn if is_res else part.shape[1]
            operands.append(part)
            if rows == TM:
                in_specs.append(pl.BlockSpec((TM, cols), (lambda j, i: (i, j)) if is_res
                                             else (lambda j, i: (i, 0))))
            else:
                in_specs.append(pl.BlockSpec((T_S, cols), (lambda j, i: (0, j)) if is_res
                                             else (lambda j, i: (0, 0))))
    return pl.pallas_call(
        functools.partial(_proj_res_body, split=tuple(split)),
        grid=(D_MODEL // tn, N_TM),
        in_specs=in_specs + [pl.BlockSpec((k_total, tn), lambda j, i: (0, j))],
        out_specs=pl.BlockSpec((TM, tn), lambda j, i: (i, j)),
        out_shape=jax.ShapeDtypeStruct((T, D_MODEL), F32),
        compiler_params=_cparams(("arbitrary", "arbitrary")),
        name=name,
    )(*operands, w)


def _ffn_body(x_hbm, g_ref, wg_ref, wu_ref, wd_ref, o_hbm, acc_ref, xn_ref, sem):
    i = pl.program_id(0)
    f = pl.program_id(1)
    rows = pl.ds(i * TM_FFN, TM_FFN)

    @pl.when(f == 0)
    def _():
        cp = pltpu.make_async_copy(x_hbm.at[rows], acc_ref, sem)
        cp.start()
        cp.wait()
        xn_ref[...] = _rms(acc_ref[...], g_ref[...]).astype(BF)

    xn = xn_ref[...]
    gate = _bdot(xn, wg_ref[...].astype(BF))
    up = _bdot(xn, wu_ref[...].astype(BF))
    act = (gate * jax.nn.sigmoid(gate) * up).astype(BF)
    acc_ref[...] += _bdot(act, wd_ref[...].astype(BF))

    @pl.when(f == pl.num_programs(1) - 1)
    def _():
        cp = pltpu.make_async_copy(acc_ref, o_hbm.at[rows], sem)
        cp.start()
        cp.wait()


def _ffn(x, g, wg, wu, wd):
    return pl.pallas_call(
        _ffn_body,
        grid=(T // TM_FFN, D_FF // TF_FFN),
        in_specs=[
            pl.BlockSpec(memory_space=pl.ANY),
            pl.BlockSpec((1, D_MODEL), lambda i, f: (0, 0)),
            pl.BlockSpec((D_MODEL, TF_FFN), lambda i, f: (0, f)),
            pl.BlockSpec((D_MODEL, TF_FFN), lambda i, f: (0, f)),
            pl.BlockSpec((TF_FFN, D_MODEL), lambda i, f: (f, 0)),
        ],
        out_specs=pl.BlockSpec(memory_space=pl.ANY),
        out_shape=jax.ShapeDtypeStruct((T, D_MODEL), F32),
        scratch_shapes=[pltpu.VMEM((TM_FFN, D_MODEL), F32),
                        pltpu.VMEM((TM_FFN, D_MODEL), BF),
                        pltpu.SemaphoreType.DMA(())],
        compiler_params=_cparams(("arbitrary", "arbitrary")),
        name="ffn",
    )(x, g, wg, wu, wd)


def _dqkv_body(x_ref, g_ref, w_ref, gq_ref, gkv_ref, cos_ref, sin_ref,
               cq_ref, ckv_ref, pe_ref, lat_ref, lat_s_ref):
    xn = _rms(x_ref[...], g_ref[...]).astype(BF)
    d = _bdot(xn, w_ref[...].astype(BF))
    cq_ref[...] = _rms(d[:, :Q_LORA], gq_ref[...]).astype(BF)
    ckv = _rms(d[:, Q_LORA:Q_LORA + KV_LORA], gkv_ref[...])
    pe = _rope128(d[:, Q_LORA + KV_LORA:], cos_ref[...], sin_ref[...])
    ckv_ref[...] = ckv.astype(BF)
    pe_ref[...] = pe.astype(BF)
    lat_ref[:, :KV_LORA] = ckv
    lat_ref[:, KV_LORA:] = pe[:, :QK_ROPE]

    @pl.when(pl.program_id(0) == N_TM - 1)
    def _():
        lat_s_ref[:, :KV_LORA] = ckv[TAIL_P:, :]
        lat_s_ref[:, KV_LORA:] = pe[TAIL_P:, :QK_ROPE]


def _dqkv(x, g, w_pad, gq, gkv, cos_t, sin_t):
    n = w_pad.shape[1]
    row = lambda i: (i, 0)
    fix = lambda i: (0, 0)
    return pl.pallas_call(
        _dqkv_body,
        grid=(N_TM,),
        in_specs=[
            pl.BlockSpec((TM, D_MODEL), row),
            pl.BlockSpec((1, D_MODEL), fix),
            pl.BlockSpec((D_MODEL, n), fix),
            pl.BlockSpec((1, Q_LORA), fix),
            pl.BlockSpec((1, KV_LORA), fix),
            pl.BlockSpec((TM, 128), row),
            pl.BlockSpec((TM, 128), row),
        ],
        out_specs=[
            pl.BlockSpec((TM, Q_LORA), row),
            pl.BlockSpec((TM, KV_LORA), row),
            pl.BlockSpec((TM, 128), row),
            pl.BlockSpec((TM, LATENT), row),
            pl.BlockSpec((T_S, LATENT), fix),
        ],
        out_shape=[
            jax.ShapeDtypeStruct((T, Q_LORA), BF),
            jax.ShapeDtypeStruct((T, KV_LORA), BF),
            jax.ShapeDtypeStruct((T, 128), BF),
            jax.ShapeDtypeStruct((T_P, LATENT), F32),
            jax.ShapeDtypeStruct((T_S, LATENT), F32),
        ],
        compiler_params=_cparams(("arbitrary",)),
        name="dqkv",
    )(x, g, w_pad, gq, gkv, cos_t, sin_t)


HG = 8


def _qkv_body(cq_ref, ckv_ref, pe_ref, cos_ref, sin_ref, wq_ref, wk_ref, wvt_ref,
              q_ref, k_ref, vt_ref):
    q = _bdot(cq_ref[...], wq_ref[...].astype(BF)) * np.float32(ATTN_SCALE * LOG2_E)
    ckv = ckv_ref[...]
    kn = _bdot(ckv, wk_ref[...].astype(BF))
    pe = pe_ref[...]
    c = cos_ref[...]
    s = sin_ref[...]
    for h in range(HG):
        o = h * HEAD_PAD
        q_ref[:, o:o + QK_NOPE] = q[:, o:o + QK_NOPE].astype(BF)
        q_ref[:, o + QK_NOPE:o + HEAD_PAD] = _rope128(q[:, o + QK_NOPE:o + HEAD_PAD], c, s).astype(BF)
        k_ref[:, o:o + QK_NOPE] = kn[:, h * QK_NOPE:(h + 1) * QK_NOPE].astype(BF)
        k_ref[:, o + QK_NOPE:o + HEAD_PAD] = pe
    vt_ref[...] = _bdot_nt(wvt_ref[...].astype(BF), ckv).astype(BF)


def _qkv(cq, ckv, pe, cos_t, sin_t, wq_pad, wk, wv_t):
    row = lambda j, i: (i, 0)
    col = lambda j, i: (0, j)
    out = lambda j, i: (i, j)
    return pl.pallas_call(
        _qkv_body,
        grid=(N_HEADS // HG, N_TM),
        in_specs=[
            pl.BlockSpec((TM, Q_LORA), row),
            pl.BlockSpec((TM, KV_LORA), row),
            pl.BlockSpec((TM, 128), row),
            pl.BlockSpec((TM, 128), row),
            pl.BlockSpec((TM, 128), row),
            pl.BlockSpec((Q_LORA, HG * HEAD_PAD), col),
            pl.BlockSpec((KV_LORA, HG * QK_NOPE), col),
            pl.BlockSpec((HG * V_HEAD, KV_LORA), lambda j, i: (j, 0)),
        ],
        out_specs=[
            pl.BlockSpec((TM, HG * HEAD_PAD), out),
            pl.BlockSpec((TM, HG * HEAD_PAD), out),
            pl.BlockSpec((HG * V_HEAD, TM), lambda j, i: (j, i)),
        ],
        out_shape=[
            jax.ShapeDtypeStruct((T, N_HEADS * HEAD_PAD), BF),
            jax.ShapeDtypeStruct((T, N_HEADS * HEAD_PAD), BF),
            jax.ShapeDtypeStruct((N_HEADS * V_HEAD, T), BF),
        ],
        compiler_params=_cparams(("arbitrary", "arbitrary")),
        name="qkv",
    )(cq, ckv, pe, cos_t, sin_t, wq_pad, wk, wv_t)


_PAIRS = [(qi, ki) for qi in range(N_QT) for ki in range(qi + 1)]


def _flash_body(qi_tab, ki_tab, q_ref, k_ref, vt_ref, o_ref, m_ref, l_ref, acc_ref):
    p = pl.program_id(1)
    qi = qi_tab[p]
    ki = ki_tab[p]

    @pl.when(ki == 0)
    def _():
        m_ref[...] = jnp.full(m_ref.shape, -jnp.inf, F32)
        l_ref[...] = jnp.zeros(l_ref.shape, F32)
        acc_ref[...] = jnp.zeros(acc_ref.shape, F32)

    def all_heads(on_diagonal):
        if on_diagonal:
            key = lax.broadcasted_iota(jnp.int32, (TQ, TQ), 0)
            qry = lax.broadcasted_iota(jnp.int32, (TQ, TQ), 1)
            visible = key <= qry
        for h in range(N_HEADS):
            qk = slice(h * HEAD_PAD, (h + 1) * HEAD_PAD)
            vs = slice(h * V_HEAD, (h + 1) * V_HEAD)
            st = _bdot_nt(k_ref[:, qk], q_ref[:, qk])
            if on_diagonal:
                st = jnp.where(visible, st, -jnp.inf)
            m_prev = m_ref[h]
            m_new = jnp.maximum(m_prev, jnp.max(st, axis=0, keepdims=True))
            alpha = jnp.exp2(m_prev - m_new)
            pt = jnp.exp2(st - m_new)
            l_ref[h] = alpha * l_ref[h] + jnp.sum(pt, axis=0, keepdims=True)
            acc_ref[vs, :] = alpha * acc_ref[vs, :] + _bdot(vt_ref[vs, :], pt.astype(BF))
            m_ref[h] = m_new

    @pl.when(ki < qi)
    def _():
        all_heads(False)

    @pl.when(ki == qi)
    def _():
        all_heads(True)
        for h in range(N_HEADS):
            vs = slice(h * V_HEAD, (h + 1) * V_HEAD)
            o_ref[:, vs] = (acc_ref[vs, :] / l_ref[h]).T.astype(BF)


def _flash(q, k, vt):
    qi_tab = jnp.asarray(np.array([p[0] for p in _PAIRS], np.int32))
    ki_tab = jnp.asarray(np.array([p[1] for p in _PAIRS], np.int32))
    grid_spec = pltpu.PrefetchScalarGridSpec(
        num_scalar_prefetch=2,
        grid=(BATCH, len(_PAIRS)),
        in_specs=[
            pl.BlockSpec((TQ, N_HEADS * HEAD_PAD), lambda b, p, qt, kt: (b * N_QT + qt[p], 0)),
            pl.BlockSpec((TQ, N_HEADS * HEAD_PAD), lambda b, p, qt, kt: (b * N_QT + kt[p], 0)),
            pl.BlockSpec((N_HEADS * V_HEAD, TQ), lambda b, p, qt, kt: (0, b * N_QT + kt[p])),
        ],
        out_specs=pl.BlockSpec((TQ, N_HEADS * V_HEAD), lambda b, p, qt, kt: (b * N_QT + qt[p], 0)),
        scratch_shapes=[
            pltpu.VMEM((N_HEADS, 1, TQ), F32),
            pltpu.VMEM((N_HEADS, 1, TQ), F32),
            pltpu.VMEM((N_HEADS * V_HEAD, TQ), F32),
        ],
    )
    return pl.pallas_call(
        _flash_body,
        grid_spec=grid_spec,
        out_shape=jax.ShapeDtypeStruct((T_P, N_HEADS * V_HEAD), BF),
        compiler_params=_cparams(("arbitrary", "arbitrary")),
        name="flash",
    )(qi_tab, ki_tab, q, k, vt)


def _qlat_body(q_ref, wk_ref, o_ref):
    o_ref[...] = _bdot_nt(q_ref[:, :QK_NOPE], wk_ref[...].astype(BF)).astype(BF)


def _qlat(q, wk):
    return pl.pallas_call(
        _qlat_body,
        grid=(N_HEADS,),
        in_specs=[
            pl.BlockSpec((T_S, HEAD_PAD), lambda h: (T_P // T_S, h)),
            pl.BlockSpec((KV_LORA, QK_NOPE), lambda h: (0, h)),
        ],
        out_specs=pl.BlockSpec((T_S, KV_LORA), lambda h: (0, h)),
        out_shape=jax.ShapeDtypeStruct((T_S, N_HEADS * KV_LORA), BF),
        compiler_params=_cparams(("arbitrary",)),
        name="qlat",
    )(q, wk)


def _decode_body(pt_ref, qlat_ref, q_ref, lat_ref, cache_hbm, o_ref,
                 buf, kb_ref, sems, m_ref, l_ref, acc_ref):
    b = pl.program_id(0)
    c = pl.program_id(1)
    step = b * N_KV_STEPS + c
    n_steps = DEC_BATCH * N_KV_STEPS
    slot = step % KV_SLOTS

    def page_copy(st, j, sl):
        pg = pt_ref[st * PAGES_PER_STEP + j]
        return pltpu.make_async_copy(cache_hbm.at[pg], buf.at[sl, j], sems.at[sl])

    def start_fetch(st, sl):
        for j in range(PAGES_PER_STEP):
            page_copy(st, j, sl).start()

    @pl.when(step == 0)
    def _():
        for s0 in range(KV_SLOTS - 1):
            start_fetch(s0, s0)

    ahead = step + (KV_SLOTS - 1)

    @pl.when(ahead < n_steps)
    def _():
        start_fetch(ahead, ahead % KV_SLOTS)

    qlat = qlat_ref[0]
    qpe = q_ref[0, :, QK_NOPE:QK_NOPE + QK_ROPE]

    @pl.when(c == 0)
    def _():
        lat = lat_ref[pl.ds(b, 1), :]
        s_self = (jnp.sum(qlat.astype(F32) * lat[:, :KV_LORA], axis=-1, keepdims=True)
                  + jnp.sum(qpe.astype(F32) * lat[:, KV_LORA:], axis=-1, keepdims=True))
        m_ref[...] = s_self
        l_ref[...] = jnp.ones(l_ref.shape, F32)
        acc_ref[...] = jnp.broadcast_to(lat[:, :KV_LORA], acc_ref.shape)

    for j in range(PAGES_PER_STEP):
        page_copy(step, j, slot).wait()

    s_parts = []
    for j in range(PAGES_PER_STEP):
        kb_ref[j] = buf[slot, j].astype(BF)
        s_parts.append(_bdot(qlat, kb_ref[j, :KV_LORA, :]) + _bdot(qpe, kb_ref[j, KV_LORA:, :]))
    s = jnp.concatenate(s_parts, axis=1)
    m_prev = m_ref[...]
    m_new = jnp.maximum(m_prev, jnp.max(s, axis=-1, keepdims=True))
    alpha = jnp.exp2(m_prev - m_new)
    pexp = jnp.exp2(s - m_new)
    l_ref[...] = alpha * l_ref[...] + jnp.sum(pexp, axis=-1, keepdims=True)
    pb = pexp.astype(BF)
    pv = jnp.zeros(acc_ref.shape, F32)
    for j in range(PAGES_PER_STEP):
        pv = pv + _bdot_nt(pb[:, j * PAGE_SIZE:(j + 1) * PAGE_SIZE], kb_ref[j, :KV_LORA, :])
    acc_ref[...] = alpha * acc_ref[...] + pv
    m_ref[...] = m_new

    @pl.when(c == N_KV_STEPS - 1)
    def _():
        o_ref[0] = acc_ref[...] / l_ref[...]


def _decode(page_table_flat, qlat3, q3, lat_s, cache_t):
    grid_spec = pltpu.PrefetchScalarGridSpec(
        num_scalar_prefetch=1,
        grid=(DEC_BATCH, N_KV_STEPS),
        in_specs=[
            pl.BlockSpec((1, N_HEADS, KV_LORA), lambda b, c, pt: (b, 0, 0)),
            pl.BlockSpec((1, N_HEADS, HEAD_PAD), lambda b, c, pt: (b, 0, 0)),
            pl.BlockSpec((T_S, LATENT), lambda b, c, pt: (0, 0)),
            pl.BlockSpec(memory_space=pl.ANY),
        ],
        out_specs=pl.BlockSpec((1, N_HEADS, KV_LORA), lambda b, c, pt: (b, 0, 0)),
        scratch_shapes=[
            pltpu.VMEM((KV_SLOTS, PAGES_PER_STEP, LATENT, PAGE_SIZE), F32),
            pltpu.VMEM((PAGES_PER_STEP, LATENT, PAGE_SIZE), BF),
            pltpu.SemaphoreType.DMA((KV_SLOTS,)),
            pltpu.VMEM((N_HEADS, 1), F32),
            pltpu.VMEM((N_HEADS, 1), F32),
            pltpu.VMEM((N_HEADS, KV_LORA), F32),
        ],
    )
    return pl.pallas_call(
        _decode_body,
        grid_spec=grid_spec,
        out_shape=jax.ShapeDtypeStruct((DEC_BATCH, N_HEADS, KV_LORA), F32),
        compiler_params=_cparams(("arbitrary", "arbitrary")),
        name="decode",
    )(page_table_flat, qlat3, q3, lat_s, cache_t)


def _ouv_body(o_ref, wv_ref, out_ref):
    out_ref[...] = _bdot(o_ref[...].astype(BF), wv_ref[...].astype(BF)).astype(BF)


def _ouv(olat2d, wv):
    return pl.pallas_call(
        _ouv_body,
        grid=(N_HEADS,),
        in_specs=[
            pl.BlockSpec((T_S, KV_LORA), lambda h: (0, h)),
            pl.BlockSpec((KV_LORA, V_HEAD), lambda h: (0, h)),
        ],
        out_specs=pl.BlockSpec((T_S, V_HEAD), lambda h: (0, h)),
        out_shape=jax.ShapeDtypeStruct((T_S, N_HEADS * V_HEAD), BF),
        compiler_params=_cparams(("arbitrary",)),
        name="ouv",
    )(olat2d, wv)


def _router_body(x_ref, g_ref, w_ref, meta_ref):
    xn = _rms(x_ref[...], g_ref[...])
    w = w_ref[...]
    xh = xn.astype(BF)
    xl = (xn - xh.astype(F32)).astype(BF)
    wh = w.astype(BF)
    wl = (w - wh.astype(F32)).astype(BF)
    logits = _bdot(xh, wh) + (_bdot(xh, wl) + _bdot(xl, wh))
    lane = lax.broadcasted_iota(jnp.int32, logits.shape, 1).astype(F32)
    lg = jnp.where(lane < N_EXPERTS, logits, -jnp.inf)
    t1 = jnp.max(lg, axis=-1, keepdims=True)
    i1 = jnp.min(jnp.where(lg == t1, lane, 128.0), axis=-1, keepdims=True)
    lg2 = jnp.where(lane == i1, -jnp.inf, lg)
    t2 = jnp.max(lg2, axis=-1, keepdims=True)
    i2 = jnp.min(jnp.where(lg2 == t2, lane, 128.0), axis=-1, keepdims=True)
    e = jnp.exp(t2 - t1)
    g1 = 1.0 / (1.0 + e)
    g2 = e / (1.0 + e)
    meta = jnp.where(lane == 0, i1, jnp.where(lane == 1, i2, jnp.where(lane == 2, g1, g2)))
    meta_ref[...] = meta


def _router(x, g, w_pad):
    return pl.pallas_call(
        _router_body,
        grid=(N_TM,),
        in_specs=[
            pl.BlockSpec((TM, D_MODEL), lambda i: (i, 0)),
            pl.BlockSpec((1, D_MODEL), lambda i: (0, 0)),
            pl.BlockSpec((D_MODEL, 128), lambda i: (0, 0)),
        ],
        out_specs=pl.BlockSpec((TM, 128), lambda i: (i, 0)),
        out_shape=jax.ShapeDtypeStruct((T, 128), F32),
        compiler_params=_cparams(("arbitrary",)),
        name="router",
    )(x, g, w_pad)


N_FF = D_FF // TF
N_CH = MOE_TM // MOE_CH


def _moe_body(te_ref, rows_ref, ybase_ref, src_ref, h_hbm, g_ref, wg_ref, wu_ref, wd_ref, y_hbm,
              acc_ref, xn_ref, wgb_ref, wub_ref, wdb_ref, zero_ref, gsems, wsem):
    t = pl.program_id(0)
    f = pl.program_id(1)
    nch = (rows_ref[t] + MOE_CH - 1) // MOE_CH
    big = MOE_BLOCKS[0]
    nbig = nch // big
    base = pl.multiple_of(ybase_ref[t], MOE_CH)

    def remainders():
        out = []
        ch0 = nbig * big
        left = nch - ch0
        for size in MOE_BLOCKS[1:]:
            present = left >= size
            out.append((ch0, size, present))
            ch0 = ch0 + jnp.where(present, size, 0)
            left = left - jnp.where(present, size, 0)
        return out

    def issue_gather(ch):
        def body(r0, carry):
            for j in range(DMA_UNROLL):
                r = ch * MOE_CH + r0 * DMA_UNROLL + j
                pltpu.make_async_copy(h_hbm.at[pl.ds(src_ref[base + r], 1)],
                                      acc_ref.at[pl.ds(r, 1)], gsems.at[ch]).start()
            return carry
        lax.fori_loop(0, MOE_CH // DMA_UNROLL, body, 0)

    def wait_gather(ch):
        pltpu.make_async_copy(h_hbm.at[pl.ds(0, MOE_CH)],
                              acc_ref.at[pl.ds(ch * MOE_CH, MOE_CH)], gsems.at[ch]).wait()

    def writeback(row0, nr):
        return pltpu.make_async_copy(acc_ref.at[pl.ds(row0, nr)],
                                     y_hbm.at[pl.ds(pl.multiple_of(base + row0, MOE_CH), nr)], wsem)

    def cast_weights():
        wg = wg_ref[0].astype(BF)
        wu = wu_ref[0].astype(BF)
        wd = wd_ref[0].astype(BF)
        wgb_ref[...] = wg
        wub_ref[...] = wu
        wdb_ref[...] = wd
        return wg, wu, wd

    def block(ch0, nr, first, last, cast=False):
        row0 = pl.multiple_of(ch0 * MOE_CH, MOE_CH)
        rs = pl.ds(row0, nr)
        if first:
            for c in range(nr // MOE_CH):
                wait_gather(ch0 + c)
            xn_ref[rs, :] = _rms(acc_ref[rs, :], g_ref[...]).astype(BF)
        xn = xn_ref[rs, :]
        wg, wu, wd = cast_weights() if cast else (wgb_ref[...], wub_ref[...], wdb_ref[...])
        gate = _bdot(xn, wg)
        up = _bdot(xn, wu)
        act = (gate * jax.nn.sigmoid(gate) * up).astype(BF)
        out = _bdot(act, wd)
        if first:
            acc_ref[rs, :] = out
        else:
            acc_ref[rs, :] += out
        if last:
            writeback(row0, nr).start()

    def sweep(first, last):
        if first:
            lax.fori_loop(0, nch, lambda ch, c: (issue_gather(ch), c)[1], 0)
        @pl.when(nbig > 0)
        def _():
            block(0, big * MOE_CH, first, last, cast=True)

        @pl.when(nbig == 0)
        def _():
            cast_weights()

        lax.fori_loop(1, nbig, lambda b, c: (block(big * b, big * MOE_CH, first, last), c)[1], 0)
        rems = remainders()
        for ch0, size, present in rems:
            @pl.when(present)
            def _():
                block(ch0, size * MOE_CH, first, last)

        if last:
            lax.fori_loop(0, nbig, lambda b, c: (writeback(0, big * MOE_CH).wait(), c)[1], 0)
            for ch0, size, present in rems:
                @pl.when(present)
                def _():
                    writeback(0, size * MOE_CH).wait()

    @pl.when(nch > 0)
    def _():
        @pl.when(f == 0)
        def _():
            sweep(True, False)

        @pl.when(jnp.logical_and(f > 0, f < N_FF - 1))
        def _():
            sweep(False, False)

        @pl.when(f == N_FF - 1)
        def _():
            sweep(False, True)

    @pl.when(jnp.logical_and(t == MOE_NT - 1, f == N_FF - 1))
    def _():
        used = (base + nch * MOE_CH) // MOE_CH
        zero_ref[...] = jnp.zeros(zero_ref.shape, F32)

        def fill(c):
            return pltpu.make_async_copy(
                zero_ref, y_hbm.at[pl.ds(pl.multiple_of(c * MOE_CH, MOE_CH), MOE_CH)], wsem)

        lax.fori_loop(used, MOE_ROWS // MOE_CH, lambda c, k: (fill(c).start(), k)[1], 0)
        lax.fori_loop(used, MOE_ROWS // MOE_CH, lambda c, k: (fill(0).wait(), k)[1], 0)


def _moe(tile_expert, tile_rows, ybase, src, h, g, wg, wu, wd):
    ff = lambda t, f, nr: jnp.where(nr[t] > 0, f, N_FF - 1)
    grid_spec = pltpu.PrefetchScalarGridSpec(
        num_scalar_prefetch=4,
        grid=(MOE_NT, N_FF),
        in_specs=[
            pl.BlockSpec(memory_space=pl.ANY),
            pl.BlockSpec((1, D_MODEL), lambda t, f, te, nr, yb, sr: (0, 0)),
            pl.BlockSpec((1, D_MODEL, TF), lambda t, f, te, nr, yb, sr: (te[t], 0, ff(t, f, nr))),
            pl.BlockSpec((1, D_MODEL, TF), lambda t, f, te, nr, yb, sr: (te[t], 0, ff(t, f, nr))),
            pl.BlockSpec((1, TF, D_MODEL), lambda t, f, te, nr, yb, sr: (te[t], ff(t, f, nr), 0)),
        ],
        out_specs=pl.BlockSpec(memory_space=pl.ANY),
        scratch_shapes=[
            pltpu.VMEM((MOE_TM, D_MODEL), F32),
            pltpu.VMEM((MOE_TM, D_MODEL), BF),
            pltpu.VMEM((D_MODEL, TF), BF),
            pltpu.VMEM((D_MODEL, TF), BF),
            pltpu.VMEM((TF, D_MODEL), BF),
            pltpu.VMEM((MOE_CH, D_MODEL), F32),
            pltpu.SemaphoreType.DMA((N_CH,)),
            pltpu.SemaphoreType.DMA(()),
        ],
    )
    return pl.pallas_call(
        _moe_body,
        grid_spec=grid_spec,
        out_shape=jax.ShapeDtypeStruct((MOE_ROWS, D_MODEL), F32),
        compiler_params=_cparams(("arbitrary", "arbitrary")),
        name="moe",
    )(tile_expert, tile_rows, ybase, src, h, g, wg, wu, wd)


def _combine_body(pos_ref, h_ref, meta_ref, gf_ref, y_hbm, yp_ref, ys_ref, yb_ref, sems):
    i = pl.program_id(0)
    slot = i % 2

    def fetch(tile, sl):
        rows_per_iter = DMA_UNROLL // TOP_K

        def issue(r0, carry):
            for j in range(rows_per_iter):
                r = r0 * rows_per_iter + j
                for k in range(TOP_K):
                    pltpu.make_async_copy(y_hbm.at[pl.ds(pos_ref[TOP_K * (tile * TC + r) + k], 1)],
                                          yb_ref.at[sl, k, pl.ds(r, 1)], sems.at[sl]).start()
            return carry
        lax.fori_loop(0, TC // rows_per_iter, issue, 0)

    @pl.when(i == 0)
    def _():
        fetch(0, 0)

    @pl.when(i + 1 < N_TC)
    def _():
        fetch(i + 1, 1 - slot)

    for k in range(TOP_K):
        pltpu.make_async_copy(y_hbm.at[pl.ds(0, TC)], yb_ref.at[slot, k], sems.at[slot]).wait()
    meta = meta_ref[...]
    out = h_ref[...] + meta[:, 2:3] * yb_ref[slot, 0] + meta[:, 3:4] * yb_ref[slot, 1]
    out = _rms(out, gf_ref[...])

    @pl.when(i < N_TC_P)
    def _():
        yp_ref[...] = out

    @pl.when(i >= N_TC_P)
    def _():
        ys_ref[...] = out


def _combine(pos, h, meta, gf, y):
    grid_spec = pltpu.PrefetchScalarGridSpec(
        num_scalar_prefetch=1,
        grid=(N_TC,),
        in_specs=[
            pl.BlockSpec((TC, D_MODEL), lambda i, pos: (i, 0)),
            pl.BlockSpec((TC, 128), lambda i, pos: (i, 0)),
            pl.BlockSpec((1, D_MODEL), lambda i, pos: (0, 0)),
            pl.BlockSpec(memory_space=pl.ANY),
        ],
        out_specs=[
            pl.BlockSpec((TC, D_MODEL), lambda i, pos: (jnp.minimum(i, N_TC_P - 1), 0)),
            pl.BlockSpec((T_S, D_MODEL), lambda i, pos: (0, 0)),
        ],
        scratch_shapes=[
            pltpu.VMEM((2, TOP_K, TC, D_MODEL), F32),
            pltpu.SemaphoreType.DMA((2,)),
        ],
    )
    return pl.pallas_call(
        _combine_body,
        grid_spec=grid_spec,
        out_shape=[
            jax.ShapeDtypeStruct((T_P, D_MODEL), F32),
            jax.ShapeDtypeStruct((T_S, D_MODEL), F32),
        ],
        compiler_params=_cparams(("arbitrary",)),
        name="combine",
    )(pos, h, meta, gf, y)


def _routing_tables(experts):
    flat = experts.reshape(-1)
    onehot = (flat[:, None] == jnp.arange(N_EXPERTS, dtype=jnp.int32)[None, :]).astype(jnp.int32)
    csum = jnp.cumsum(onehot, axis=0)
    rank = jnp.take_along_axis(csum, flat[:, None], axis=1)[:, 0] - 1
    counts = csum[-1]
    tiles_per = (counts + MOE_TM - 1) // MOE_TM
    tile_end = jnp.cumsum(tiles_per)
    tile_start = tile_end - tiles_per
    n_used = tile_end[-1]
    tid = jnp.arange(MOE_NT, dtype=jnp.int32)
    last_tile = jnp.maximum(n_used - 1, 0)
    te = jnp.sum(tile_end[None, :] <= jnp.minimum(tid, last_tile)[:, None], axis=1).astype(jnp.int32)
    te = jnp.minimum(te, N_EXPERTS - 1)
    rows = jnp.clip(counts[te] - (tid - tile_start[te]) * MOE_TM, 0, MOE_TM)
    rows = jnp.where(tid < n_used, rows, 0).astype(jnp.int32)
    rows_pad = (rows + MOE_CH - 1) // MOE_CH * MOE_CH
    ybase = (jnp.cumsum(rows_pad) - rows_pad).astype(jnp.int32)
    pos = (ybase[tile_start[flat] + rank // MOE_TM] + rank % MOE_TM).astype(jnp.int32)
    return te, rows, ybase, _gather_sources(pos), pos


def _gather_sources_body(pos_ref, src_ref):
    per_iter = DMA_UNROLL

    def clear(i, carry):
        for j in range(per_iter):
            src_ref[i * per_iter + j] = jnp.int32(0)
        return carry

    def put(i, carry):
        for j in range(per_iter):
            src_ref[pos_ref[i * per_iter + j]] = i * (per_iter // TOP_K) + j // TOP_K
        return carry

    lax.fori_loop(0, MOE_ROWS // per_iter, clear, 0)
    lax.fori_loop(0, (T * TOP_K) // per_iter, put, 0)


def _gather_sources(pos):
    assert MOE_ROWS % DMA_UNROLL == 0 and (T * TOP_K) % DMA_UNROLL == 0 and DMA_UNROLL % TOP_K == 0
    return pl.pallas_call(
        _gather_sources_body,
        in_specs=[pl.BlockSpec(memory_space=pltpu.SMEM)],
        out_specs=pl.BlockSpec(memory_space=pltpu.SMEM),
        out_shape=jax.ShapeDtypeStruct((MOE_ROWS,), jnp.int32),
        name="gather_sources",
    )(pos)


def _rope_tables():
    half = QK_ROPE // 2
    inv = 1.0 / (ROPE_THETA ** (jnp.arange(half, dtype=F32) * 2.0 / QK_ROPE))

    def tables(pos):
        ang = pos.astype(F32)[:, None] * inv[None, :]
        cos, sin = jnp.cos(ang), jnp.sin(ang)
        zero = jnp.zeros((pos.shape[0], 128 - QK_ROPE), F32)
        return (jnp.concatenate([cos, cos, zero], axis=1),
                jnp.concatenate([-sin, sin, zero], axis=1))

    cos_p, sin_p = tables(jnp.arange(SEQ))
    cos_s, sin_s = tables(jnp.full((1,), PAST_LEN))
    stack = lambda p, s: jnp.concatenate([jnp.tile(p, (BATCH, 1)), jnp.broadcast_to(s, (T_S, 128))])
    return stack(cos_p, cos_s), stack(sin_p, sin_s)


def kernel(x_prompt, x_sample, state_pool, cache_mla, page_table, norm_mix, norm_ffn, w_in, g_v, w_s, b_s, w_pool, pool_scale, w_o_mix, w_ffn_gate, w_ffn_up, w_ffn_down, w_dqkv, g_q, g_kv, w_uq, w_uk, w_uv, w_o_attn, w_router, w_exp_gate, w_exp_up, w_exp_down, norm_final):
    x_p = x_prompt.reshape(T_P, D_MODEL)
    x_s = x_sample.reshape(T_S, D_MODEL)

    y = _inproj(x_p, x_s, norm_mix[0:1], w_in[0], g_v)
    b_bcast = jnp.broadcast_to(b_s[0][:, :, None], (A_HEADS, CHUNK, CHUNK))
    a_out = _gate(y, w_s[0], b_bcast)
    hist2d = state_pool[0].reshape(T_S, POOL_HIST * B_WIDTH)
    b_out = _pool(y, hist2d, w_pool[0], pool_scale[0])
    h = _proj_res([a_out, b_out], w_o_mix[0], (x_p, x_s), "mix_out")
    h = _ffn(h, norm_ffn[0:1], w_ffn_gate[0], w_ffn_up[0], w_ffn_down[0])

    pool_state_prompt = jnp.stack(
        [y[2, (b + 1) * SEQ - POOL_HIST:(b + 1) * SEQ] for b in range(BATCH)])[None]
    z_s = y[2, T_P:]
    pool_state_sample = jnp.concatenate([state_pool[0][:, 1:], z_s[:, None, :]], axis=1)[None]
    chunk_v_sample = y[1, T_P:].reshape(1, T_S, 1, A_WIDTH)

    cos_t, sin_t = _rope_tables()
    w_dqkv_pad = jnp.pad(w_dqkv[0], ((0, 0), (0, 128 - QK_ROPE)))
    cq, ckv, pe, lat_p, lat_s = _dqkv(h, norm_mix[1:2], w_dqkv_pad, g_q, g_kv, cos_t, sin_t)
    wq_pad = jnp.pad(w_uq[0].reshape(Q_LORA, N_HEADS, QK_NOPE + QK_ROPE),
                     ((0, 0), (0, 0), (0, HEAD_PAD - QK_NOPE - QK_ROPE))).reshape(Q_LORA, N_HEADS * HEAD_PAD)
    wk2d = w_uk[0].reshape(KV_LORA, N_HEADS * QK_NOPE)
    wv2d = w_uv[0].reshape(KV_LORA, N_HEADS * V_HEAD)
    q, k, vt = _qkv(cq, ckv, pe, cos_t, sin_t, wq_pad, wk2d, wv2d.T)
    attn = _flash(q, k, vt)

    qlat = _qlat(q, wk2d)
    cache_t = jnp.swapaxes(cache_mla, 2, 3).reshape(-1, LATENT, PAGE_SIZE)
    olat = _decode(page_table.reshape(-1),
                   qlat.reshape(T_S, N_HEADS, KV_LORA),
                   q[T_P:].reshape(T_S, N_HEADS, HEAD_PAD),
                   lat_s,
                   cache_t)
    attn_s = _ouv(olat.reshape(T_S, N_HEADS * KV_LORA), wv2d)
    h = _proj_res([(attn, attn_s)], w_o_attn[0], h, "attn_out")

    mla_rows_prompt = lat_p.reshape(1, BATCH, SEQ, LATENT)
    mla_rows_sample = lat_s.reshape(1, T_S, 1, LATENT)

    w_router_pad = jnp.pad(w_router[0], ((0, 0), (0, 128 - N_EXPERTS)))
    meta = _router(h, norm_ffn[1:2], w_router_pad)
    experts = meta[:, :TOP_K].astype(jnp.int32)
    te, rows, ybase, src, pos = _routing_tables(experts)
    y_sorted = _moe(te, rows, ybase, src, h, norm_ffn[1:2],
                    w_exp_gate[0], w_exp_up[0], w_exp_down[0])
    y_p, y_s = _combine(pos, h, meta, norm_final.reshape(1, D_MODEL), y_sorted)

    return (y_p.reshape(BATCH, SEQ, D_MODEL), y_s.reshape(T_S, 1, D_MODEL),
            pool_state_prompt, pool_state_sample, chunk_v_sample,
            mla_rows_prompt, mla_rows_sample)
```
